```python
import jax, jax.numpy as jnp
from jax import lax
import numpy as np

D_MODEL = 1024
BATCH = 2
SEQ = 16384
DEPTH = 1
DEC_BATCH = 128
DEC_SEQ = 4
PAST_LEN = 8192
PAGE_SIZE = 128

N_MEM = 256
GLA_HEADS = 4
GLA_DK = D_MODEL // 2 // GLA_HEADS
GLA_DV = D_MODEL // GLA_HEADS
GLA_RANK = 16
GLA_GATE_NORM = 16.0
GLA_CHUNK = 64
DIL_PATTERNS = ((128, 1), (512, 4), (2048, 16))
DIL_HEADS = 4
DIL_HD = 128
Q_BLOCK = 128
X_HEADS = 4
X_HD = 128
D_FF = 4 * D_MODEL
ROPE_THETA = 10000.0
EPS = 1e-6
NEG = -1e30
SPLIT_SIZES = (GLA_HEADS * GLA_DK, GLA_HEADS * GLA_DK, GLA_HEADS * GLA_DV, GLA_HEADS * GLA_DV, GLA_RANK) + (DIL_HEADS * DIL_HD,) * (3 * len(DIL_PATTERNS)) + (X_HEADS * X_HD, 3 * D_MODEL)
D_IN = sum(SPLIT_SIZES)

kernel_name = 'gla_dilated_xattn_hybrid_step'


def rmsnorm(x, g):
    xf = x.astype(jnp.float32)
    r = lax.rsqrt(jnp.mean(xf * xf, axis=-1, keepdims=True) + EPS)
    return (xf * r).astype(x.dtype) * g


def rope(x, pos):
    half = x.shape[-1] // 2
    inv = ROPE_THETA ** (-jnp.arange(half, dtype=jnp.float32) / half)
    ang = pos.astype(jnp.float32)[:, None] * inv[None, :]
    cos = jnp.cos(ang)[None, :, None, :]
    sin = jnp.sin(ang)[None, :, None, :]
    xf = x.astype(jnp.float32)
    x1, x2 = xf[..., :half], xf[..., half:]
    return jnp.concatenate([x1 * cos - x2 * sin, x2 * cos + x1 * sin], axis=-1).astype(x.dtype)


def project_inputs(n, pos, w_in, b_gate, w_decay, b_decay):
    B, L = n.shape[0], n.shape[1]
    idx = np.cumsum(SPLIT_SIZES)[:-1].tolist()
    parts = jnp.split(n @ w_in, idx, axis=-1)
    gq, gk, gv, gr, ga = parts[:5]
    q = gq.reshape(B, L, GLA_HEADS, GLA_DK) * (GLA_DK ** -0.5)
    k = gk.reshape(B, L, GLA_HEADS, GLA_DK)
    v = gv.reshape(B, L, GLA_HEADS, GLA_DV)
    logf = jax.nn.log_sigmoid((ga @ w_decay + b_decay).astype(jnp.float32)) / GLA_GATE_NORM
    logf = logf.reshape(B, L, GLA_HEADS, GLA_DK)
    dil = []
    for g in range(len(DIL_PATTERNS)):
        dq, dkey, dv = parts[5 + 3 * g: 8 + 3 * g]
        dq = rope(dq.reshape(B, L, DIL_HEADS, DIL_HD), pos) * (DIL_HD ** -0.5)
        dkey = rope(dkey.reshape(B, L, DIL_HEADS, DIL_HD), pos)
        dil.append((dq, dkey, dv.reshape(B, L, DIL_HEADS, DIL_HD)))
    xq = parts[-2].reshape(B, L, X_HEADS, X_HD) * (X_HD ** -0.5)
    gates = jax.nn.sigmoid(parts[-1] + b_gate)
    return (q, k, v, logf), gr, dil, xq, gates


def gla_chunk(q, k, v, logf, S):
    qf, kf, vf = q.astype(jnp.float32), k.astype(jnp.float32), v.astype(jnp.float32)
    Sf = S.astype(jnp.float32)
    C = q.shape[1]
    cum = jnp.cumsum(logf, axis=1)
    causal = jnp.tril(jnp.ones((C, C), dtype=bool))
    diff = cum[:, :, None] - cum[:, None, :]
    decay = jnp.exp(jnp.where(causal[None, :, :, None, None], diff, NEG))
    att = jnp.einsum('bthc,bshc,btshc->bhts', qf, kf, decay)
    o = jnp.einsum('bhts,bshv->bthv', att, vf) + jnp.einsum('bthc,bhcv->bthv', qf * jnp.exp(cum), Sf)
    last = cum[:, -1]
    S_new = jnp.exp(last)[..., None] * Sf + jnp.einsum('bshc,bshv->bhcv', kf * jnp.exp(last[:, None] - cum), vf)
    return o.astype(v.dtype), S_new


def gla_prompt(q, k, v, logf):
    B, L, H, dk = q.shape
    nc = L // GLA_CHUNK

    def to_chunks(t):
        return t.reshape(B, nc, GLA_CHUNK, *t.shape[2:]).swapaxes(0, 1)

    def step(S, c):
        o, S = gla_chunk(c[0], c[1], c[2], c[3], S)
        return S, o

    S0 = jnp.zeros((B, H, dk, GLA_DV), jnp.float32)
    S, o = lax.scan(step, S0, (to_chunks(q), to_chunks(k), to_chunks(v), to_chunks(logf)))
    return o.swapaxes(0, 1).reshape(B, L, H, GLA_DV), S


def strided_attend(q, k_slab, v_slab, q_offset, valid_start, window, dilation):
    nq = q.shape[1]
    n_keys = window // dilation + 1
    idx = q_offset + jnp.arange(nq)[:, None] - dilation * jnp.arange(n_keys)[None, :]
    valid = idx >= valid_start
    idx = jnp.maximum(idx, 0)
    kg = k_slab[:, idx]
    vg = v_slab[:, idx]
    s = jnp.einsum('bqhd,bqkhd->bhqk', q, kg).astype(jnp.float32)
    s = jnp.where(valid[None, None], s, NEG)
    m = jnp.max(s, axis=-1)
    p = jnp.exp(s - m[..., None])
    den = jnp.sum(p, axis=-1)
    o = jnp.einsum('bhqk,bqkhd->bqhd', p, vg.astype(jnp.float32)) / den.transpose(0, 2, 1)[..., None]
    lse = (m + jnp.log(den)).transpose(0, 2, 1)
    return o.astype(q.dtype), lse


def dilated_prompt(q, k, v, window, dilation):
    B, L, H, hd = q.shape
    kp = jnp.pad(k, ((0, 0), (window, 0), (0, 0), (0, 0)))
    vp = jnp.pad(v, ((0, 0), (window, 0), (0, 0), (0, 0)))
    starts = jnp.arange(L // Q_BLOCK) * Q_BLOCK

    def block(s):
        qb = lax.dynamic_slice_in_dim(q, s, Q_BLOCK, axis=1)
        kb = lax.dynamic_slice_in_dim(kp, s, window + Q_BLOCK, axis=1)
        vb = lax.dynamic_slice_in_dim(vp, s, window + Q_BLOCK, axis=1)
        return strided_attend(qb, kb, vb, window, window - s, window, dilation)

    o, lse = lax.map(block, starts)
    return o.swapaxes(0, 1).reshape(B, L, H, hd), lse.swapaxes(0, 1).reshape(B, L, H)


def combine_groups(outs):
    o = jnp.stack([t[0] for t in outs])
    lse = jnp.stack([t[1] for t in outs])
    w = jax.nn.softmax(lse, axis=0)
    return jnp.einsum('gblh,gblhd->blhd', w, o.astype(jnp.float32)).astype(o.dtype)


def memory_kv(mem, norm_mem, w_mem_kv):
    B, M = mem.shape[0], mem.shape[1]
    mk, mv = jnp.split(rmsnorm(mem, norm_mem) @ w_mem_kv, 2, axis=-1)
    return mk.reshape(B, M, X_HEADS, X_HD), mv.reshape(B, M, X_HEADS, X_HD)


def cross_attend(q, mk, mv):
    s = jnp.einsum('blhd,bmhd->bhlm', q, mk).astype(jnp.float32)
    p = jax.nn.softmax(s, axis=-1)
    return jnp.einsum('bhlm,bmhd->blhd', p, mv.astype(jnp.float32)).astype(q.dtype)


def merge_and_ffn(x, o_gla, gr, o_dil, o_x, gates, gla_head_norm, w_proj_gla, w_proj_dil, w_proj_x, w_out, norm_ffn, w_ffn_up, w_ffn_down):
    B, L = x.shape[0], x.shape[1]
    a = rmsnorm(o_gla, gla_head_norm).reshape(B, L, -1) * jax.nn.silu(gr)
    y_a = a @ w_proj_gla
    y_b = o_dil.reshape(B, L, -1) @ w_proj_dil
    y_c = o_x.reshape(B, L, -1) @ w_proj_x
    g_a, g_b, g_c = jnp.split(gates, 3, axis=-1)
    h = x + (g_a * y_a + g_b * y_b + g_c * y_c) @ w_out
    u = rmsnorm(h, norm_ffn) @ w_ffn_up
    return h + jnp.square(jax.nn.relu(u)) @ w_ffn_down


def setup_inputs(seed: int = 0) -> dict:
    key = jax.random.key(seed)
    ks = iter(jax.random.split(key, 32))

    def nrm(shape, scale=1.0):
        return scale * jax.random.normal(next(ks), shape, jnp.float32)

    def gain(shape):
        return 1.0 + 0.05 * nrm(shape)

    def dil_shape(w):
        return (DEPTH, DEC_BATCH, min(w, PAST_LEN), DIL_HEADS, DIL_HD)

    mem_shape = (DEPTH, DEC_BATCH, N_MEM, X_HEADS, X_HD)
    return {
        'x_prompt': nrm((BATCH, SEQ, D_MODEL)),
        'x_sample': nrm((DEC_BATCH, DEC_SEQ, D_MODEL)),
        'mem_prompt': nrm((BATCH, N_MEM, D_MODEL)),
        'state_gla': nrm((DEPTH, DEC_BATCH, GLA_HEADS, GLA_DK, GLA_DV), 0.5),
        'cache_dil1_k': nrm(dil_shape(DIL_PATTERNS[0][0])),
        'cache_dil1_v': nrm(dil_shape(DIL_PATTERNS[0][0])),
        'cache_dil2_k': nrm(dil_shape(DIL_PATTERNS[1][0])),
        'cache_dil2_v': nrm(dil_shape(DIL_PATTERNS[1][0])),
        'cache_dil3_k': nrm(dil_shape(DIL_PATTERNS[2][0])),
        'cache_dil3_v': nrm(dil_shape(DIL_PATTERNS[2][0])),
        'cache_mem_k': nrm(mem_shape),
        'cache_mem_v': nrm(mem_shape),
        'norm_mix': gain((DEPTH, D_MODEL)),
        'w_in': nrm((DEPTH, D_MODEL, D_IN), D_MODEL ** -0.5),
        'b_gate': nrm((DEPTH, 3 * D_MODEL), 0.02),
        'w_decay': nrm((DEPTH, GLA_RANK, GLA_HEADS * GLA_DK), GLA_RANK ** -0.5),
        'b_decay': 2.0 + 0.1 * nrm((DEPTH, GLA_HEADS * GLA_DK)),
        'gla_head_norm': gain((DEPTH, GLA_DV)),
        'w_proj_gla': nrm((DEPTH, GLA_HEADS * GLA_DV, D_MODEL), (GLA_HEADS * GLA_DV) ** -0.5),
        'w_proj_dil': nrm((DEPTH, DIL_HEADS * DIL_HD, D_MODEL), (DIL_HEADS * DIL_HD) ** -0.5),
        'w_proj_x': nrm((DEPTH, X_HEADS * X_HD, D_MODEL), (X_HEADS * X_HD) ** -0.5),
        'norm_mem': gain((DEPTH, D_MODEL)),
        'w_mem_kv': nrm((DEPTH, D_MODEL, 2 * X_HEADS * X_HD), D_MODEL ** -0.5),
        'w_out': nrm((DEPTH, D_MODEL, D_MODEL), D_MODEL ** -0.5),
        'norm_ffn': gain((DEPTH, D_MODEL)),
        'w_ffn_up': nrm((DEPTH, D_MODEL, D_FF), D_MODEL ** -0.5),
        'w_ffn_down': nrm((DEPTH, D_FF, D_MODEL), D_FF ** -0.5),
        'norm_final': gain((D_MODEL,)),
    }


def reference(x_prompt, x_sample, mem_prompt, state_gla, cache_dil1_k, cache_dil1_v, cache_dil2_k, cache_dil2_v, cache_dil3_k, cache_dil3_v, cache_mem_k, cache_mem_v, norm_mix, w_in, b_gate, w_decay, b_decay, gla_head_norm, w_proj_gla, w_proj_dil, w_proj_x, norm_mem, w_mem_kv, w_out, norm_ffn, w_ffn_up, w_ffn_down, norm_final):
    pos_p = jnp.arange(x_prompt.shape[1])
    pos_s = PAST_LEN + jnp.arange(x_sample.shape[1])
    hp, hs = x_prompt, x_sample
    p_layers, s_layers = [], []
    for i in range(DEPTH):
        post_w = (gla_head_norm[i], w_proj_gla[i], w_proj_dil[i], w_proj_x[i], w_out[i], norm_ffn[i], w_ffn_up[i], w_ffn_down[i])
        n = rmsnorm(hp, norm_mix[i])
        (gq, gk, gv, glf), gr, dil, xq, gates = project_inputs(n, pos_p, w_in[i], b_gate[i], w_decay[i], b_decay[i])
        o_gla, S_p = gla_prompt(gq, gk, gv, glf)
        outs, bufs = [], []
        for (win, dl), (dq, dkey, dv) in zip(DIL_PATTERNS, dil):
            outs.append(dilated_prompt(dq, dkey, dv, win, dl))
            keep = min(win, dkey.shape[1])
            bufs += [dkey[:, -keep:], dv[:, -keep:]]
        o_dil = combine_groups(outs)
        mk, mv = memory_kv(mem_prompt, norm_mem[i], w_mem_kv[i])
        o_x = cross_attend(xq, mk, mv)
        hp = merge_and_ffn(hp, o_gla, gr, o_dil, o_x, gates, *post_w)
        p_layers.append((S_p.astype(hp.dtype), *bufs, mk, mv))
        n = rmsnorm(hs, norm_mix[i])
        (gq, gk, gv, glf), gr, dil, xq, gates = project_inputs(n, pos_s, w_in[i], b_gate[i], w_decay[i], b_decay[i])
        o_gla, S_s = gla_chunk(gq, gk, gv, glf, state_gla[i])
        dil_caches = ((cache_dil1_k[i], cache_dil1_v[i]), (cache_dil2_k[i], cache_dil2_v[i]), (cache_dil3_k[i], cache_dil3_v[i]))
        outs, bufs = [], []
        for (win, dl), (dq, dkey, dv), (ck, cv) in zip(DIL_PATTERNS, dil, dil_caches):
            k_slab = jnp.concatenate([ck.astype(dkey.dtype), dkey], axis=1)
            v_slab = jnp.concatenate([cv.astype(dv.dtype), dv], axis=1)
            outs.append(strided_attend(dq, k_slab, v_slab, ck.shape[1], 0, win, dl))
            keep = min(win, k_slab.shape[1])
            bufs += [k_slab[:, -keep:], v_slab[:, -keep:]]
        o_dil = combine_groups(outs)
        o_x = cross_attend(xq, cache_mem_k[i], cache_mem_v[i])
        hs = merge_and_ffn(hs, o_gla, gr, o_dil, o_x, gates, *post_w)
        s_layers.append((S_s.astype(state_gla.dtype), *bufs))
    y_prompt = rmsnorm(hp, norm_final)
    y_sample = rmsnorm(hs, norm_final)
    gla_p, d1k_p, d1v_p, d2k_p, d2v_p, d3k_p, d3v_p, mem_k_p, mem_v_p = [jnp.stack(t) for t in zip(*p_layers)]
    gla_s, d1k_s, d1v_s, d2k_s, d2v_s, d3k_s, d3v_s = [jnp.stack(t) for t in zip(*s_layers)]
    return (y_prompt, y_sample, gla_p, d1k_p, d1v_p, d2k_p, d2v_p, d3k_p, d3v_p, mem_k_p, mem_v_p, gla_s, d1k_s, d1v_s, d2k_s, d2v_s, d3k_s, d3v_s)
```

```python
import functools

import jax
import jax.numpy as jnp
from jax import lax
from jax.experimental import pallas as pl
from jax.experimental.pallas import tpu as pltpu

F32 = jnp.float32
BF16 = jnp.bfloat16

D_MODEL = 1024
PAST_LEN = 8192
N_MEM = 256
HEADS = 4
HEAD_DIM = 128
GLA_DV = 256
GLA_RANK = 16
GLA_GATE_NORM = 16.0
DIL_PATTERNS = ((128, 1), (512, 4), (2048, 16))
WINDOW_KEYS = 128
D_FF = 4 * D_MODEL
ROPE_THETA = 10000.0
EPS = 1e-6
NEG = -1e30
QK_SCALE = HEAD_DIM ** -0.5

LANES = 128
COL_CHUNK = 512
VMEM_LIMIT = 48 * 1024 * 1024


def _mm(a, b):
    return jnp.dot(a, b, preferred_element_type=F32)


def _mm_nt(a, b):
    return lax.dot_general(a, b, (((1,), (1,)), ((), ())), preferred_element_type=F32)


def _mm_tn(a, b):
    return lax.dot_general(a, b, (((0,), (0,)), ((), ())), preferred_element_type=F32)


def _rms(x, gain):
    return x * lax.rsqrt(jnp.mean(x * x, axis=-1, keepdims=True) + EPS) * gain


def _params(sem):
    return pltpu.CompilerParams(dimension_semantics=sem, vmem_limit_bytes=VMEM_LIMIT)


def _proj_body(*refs, plan, first_tail_tile):
    x_ref, g_ref, cos_ref, sin_ref, b_ref, wdec_ref, bdec_ref = refs[:7]
    w_refs = refs[7:7 + len(plan)]
    out_refs = list(refs[7 + len(plan):])
    n = _rms(x_ref[...], g_ref[...]).astype(BF16)
    in_tail = pl.program_id(1) >= first_tail_tile
    oi = 0
    for (kind, scale, width, boff, has_tail), w_ref in zip(plan, w_refs):
        out = out_refs[oi]
        oi += 1
        tail = None
        if has_tail:
            tail = out_refs[oi]
            oi += 1
        if kind == "decay":
            ga = _mm(n, w_ref[...]).astype(BF16)
            z = _mm(ga, wdec_ref[...]) + bdec_ref[...]
            logsig = jnp.minimum(z, 0.0) - jnp.log1p(jnp.exp(-jnp.abs(z)))
            out[...] = logsig * (1.0 / GLA_GATE_NORM)
            continue
        for c0 in range(0, width, COL_CHUNK):
            cw = min(COL_CHUNK, width - c0)
            y = _mm(n, w_ref[:, c0:c0 + cw])
            if kind == "rope":
                cosv = cos_ref[...]
                sinv = sin_ref[...]
                heads = []
                for j in range(cw // HEAD_DIM):
                    yh = y[:, j * HEAD_DIM:(j + 1) * HEAD_DIM]
                    heads.append(yh * cosv + pltpu.roll(yh, HEAD_DIM // 2, 1) * sinv)
                y = jnp.concatenate(heads, axis=1)
            elif kind == "sig":
                y = jax.nn.sigmoid(y + b_ref[:, boff + c0:boff + c0 + cw])
            if tail is not None:
                @pl.when(in_tail)
                def _():
                    tail[:, c0:c0 + cw] = y
            if scale != 1.0:
                y = y * scale
            out[:, c0:c0 + cw] = y.astype(out.dtype)


def _project(x, gain, cos, sin, segs, *, tm, tail_rows=0, bias=None, wdec=None, bdec=None):
    B, L, D = x.shape
    nL = L // tm
    ntail = tail_rows // tm
    first_tail_tile = nL - ntail
    if bias is None:
        bias = jnp.zeros((1, LANES), F32)
    if wdec is None:
        wdec = jnp.zeros((LANES, LANES), BF16)
        bdec = jnp.zeros((1, LANES), F32)

    def const2(shape):
        return pl.BlockSpec(shape, lambda b, i: (0, 0))

    in_specs = [
        pl.BlockSpec((None, tm, D), lambda b, i: (b, i, 0)),
        const2((1, D)),
        pl.BlockSpec((tm, LANES), lambda b, i: (i, 0)),
        pl.BlockSpec((tm, LANES), lambda b, i: (i, 0)),
        const2(bias.shape), const2(wdec.shape), const2(bdec.shape),
    ]
    plan, weights, out_shapes, out_specs = [], [], [], []
    for kind, w, scale, dtype, boff, has_tail in segs:
        width = w.shape[1]
        owidth = wdec.shape[1] if kind == "decay" else width
        plan.append((kind, scale, width, boff, has_tail))
        weights.append(w)
        in_specs.append(const2(w.shape))
        out_shapes.append(jax.ShapeDtypeStruct((B, L, owidth), dtype))
        out_specs.append(pl.BlockSpec((None, tm, owidth), lambda b, i: (b, i, 0)))
        if has_tail:
            out_shapes.append(jax.ShapeDtypeStruct((B, tail_rows, width), F32))
            out_specs.append(pl.BlockSpec(
                (None, tm, width),
                lambda b, i: (b, jnp.maximum(i - first_tail_tile, 0), 0)))
    body = functools.partial(_proj_body, plan=tuple(plan), first_tail_tile=first_tail_tile)
    return pl.pallas_call(
        body, out_shape=out_shapes, grid=(B, nL), in_specs=in_specs, out_specs=out_specs,
        compiler_params=_params(("parallel", "arbitrary")), name="project",
    )(x, gain, cos, sin, bias, wdec, bdec, *weights)


def _gla_head(q, k, v, lf, s0, sub):
    T = q.shape[0]
    row = lax.broadcasted_iota(jnp.int32, (T, T), 0)
    col = lax.broadcasted_iota(jnp.int32, (T, T), 1)
    tri = (row >= col).astype(F32)
    cum = jnp.dot(tri, lf, precision=lax.Precision.HIGHEST, preferred_element_type=F32)
    qf = q.astype(F32)
    kf = k.astype(F32)
    last = cum[T - 1:T, :]
    o = _mm((qf * jnp.exp(cum)).astype(BF16), s0.astype(BF16))
    lane = lax.broadcasted_iota(jnp.int32, (sub, T), 1)
    trow = lax.broadcasted_iota(jnp.int32, (sub, T), 0)
    blocks = []
    for i in range(T // sub):
        r0 = i * sub
        qi = qf[r0:r0 + sub]
        ki = kf[r0:r0 + sub]
        ci = cum[r0:r0 + sub]
        if i > 0:
            bound = cum[r0 - 1:r0, :]
            qd = (qi * jnp.exp(ci - bound)).astype(BF16)
            kd = (kf * jnp.exp(jnp.minimum(bound - cum, 0.0))).astype(BF16)
            att = jnp.where(lane < r0, _mm_nt(qd, kd), 0.0)
        else:
            att = jnp.zeros((sub, T), F32)
        for s in range(sub):
            dec = jnp.exp(jnp.minimum(ci - ci[s:s + 1, :], 0.0))
            pair = jnp.sum(qi * ki[s:s + 1, :] * dec, axis=-1, keepdims=True)
            att = jnp.where((lane == r0 + s) & (trow >= s), pair, att)
        blocks.append(att)
    att = blocks[0] if len(blocks) == 1 else jnp.concatenate(blocks, axis=0)
    o = o + _mm(att.astype(BF16), v)
    upd = _mm_tn((kf * jnp.exp(last - cum)).astype(BF16), v)
    er = lax.broadcasted_iota(jnp.int32, (HEAD_DIM, HEAD_DIM), 0)
    ec = lax.broadcasted_iota(jnp.int32, (HEAD_DIM, HEAD_DIM), 1)
    decay_col = jnp.sum(jnp.where(er == ec, jnp.exp(last), 0.0), axis=1, keepdims=True)
    return o, decay_col * s0 + upd


def _gla_body(q_ref, k_ref, v_ref, lf_ref, r_ref, s0_ref, hn_ref, a_ref, s_ref, *, nb, sub):
    first = pl.program_id(1) == 0
    for b in range(nb):
        for h in range(HEADS):
            @pl.when(first)
            def _():
                s_ref[b, h] = s0_ref[b, h]
            dk = slice(h * HEAD_DIM, (h + 1) * HEAD_DIM)
            dv = slice(h * GLA_DV, (h + 1) * GLA_DV)
            o, s_new = _gla_head(q_ref[b, :, dk], k_ref[b, :, dk], v_ref[b, :, dv],
                                 lf_ref[b, :, dk], s_ref[b, h], sub)
            s_ref[b, h] = s_new
            r = r_ref[b, :, dv].astype(F32)
            a_ref[b, :, dv] = (_rms(o, hn_ref[...]) * (r * jax.nn.sigmoid(r))).astype(a_ref.dtype)


def _gla(q, k, v, lf, r, s0, head_norm, *, chunk, sub, nb):
    B, L, _ = q.shape
    tok = lambda w: pl.BlockSpec((nb, chunk, w), lambda b, c: (b, c, 0))
    state = pl.BlockSpec((nb, HEADS, HEAD_DIM, GLA_DV), lambda b, c: (b, 0, 0, 0))
    return pl.pallas_call(
        functools.partial(_gla_body, nb=nb, sub=sub),
        out_shape=[jax.ShapeDtypeStruct((B, L, HEADS * GLA_DV), BF16),
                   jax.ShapeDtypeStruct(s0.shape, F32)],
        grid=(B // nb, L // chunk),
        in_specs=[tok(512), tok(512), tok(1024), tok(512), tok(1024), state,
                  pl.BlockSpec((1, GLA_DV), lambda b, c: (0, 0))],
        out_specs=[tok(1024), state],
        compiler_params=_params(("parallel", "arbitrary")), name="gla",
    )(q, k, v, lf, r, s0, head_norm)


def _band_body(q_ref, k_ref, v_ref, kp_ref, vp_ref, o_ref, lse_ref, *, tq):
    step = pl.program_id(2)
    nblk = tq // WINDOW_KEYS
    t = lax.broadcasted_iota(jnp.int32, (WINDOW_KEYS, 2 * WINDOW_KEYS), 0)
    c = lax.broadcasted_iota(jnp.int32, (WINDOW_KEYS, 2 * WINDOW_KEYS), 1)
    band = (c >= t) & (c <= t + WINDOW_KEYS)
    lane = lax.broadcasted_iota(jnp.int32, (WINDOW_KEYS, LANES), 1)
    for j in range(nblk):
        rows = slice(j * WINDOW_KEYS, (j + 1) * WINDOW_KEYS)
        prev = slice((j - 1) * WINDOW_KEYS, j * WINDOW_KEYS)
        first_key = step * tq + (j - 1) * WINDOW_KEYS
        valid = band & (c + first_key >= 0)
        lse_blk = jnp.zeros((WINDOW_KEYS, LANES), F32)
        for h in range(HEADS):
            hd = slice(h * HEAD_DIM, (h + 1) * HEAD_DIM)
            if j == 0:
                kk = jnp.concatenate([kp_ref[:, hd], k_ref[rows, hd]], axis=0)
                vv = jnp.concatenate([vp_ref[:, hd], v_ref[rows, hd]], axis=0)
            else:
                kk = jnp.concatenate([k_ref[prev, hd], k_ref[rows, hd]], axis=0)
                vv = jnp.concatenate([v_ref[prev, hd], v_ref[rows, hd]], axis=0)
            s = jnp.where(valid, _mm_nt(q_ref[rows, hd], kk), NEG)
            m = jnp.max(s, axis=-1, keepdims=True)
            p = jnp.exp(s - m)
            den = jnp.sum(p, axis=-1, keepdims=True)
            o_ref[rows, hd] = (_mm(p.astype(BF16), vv) / den).astype(o_ref.dtype)
            lse_blk = jnp.where(lane == h, m + jnp.log(den), lse_blk)
        lse_ref[rows, :] = lse_blk[:, :HEADS]


def _band_attention(q, k, v, dil, *, tq):
    B, L, W = q.shape
    n = L // dil
    view = lambda a: a.reshape(B, n, dil * W)
    per = tq // WINDOW_KEYS
    cur = pl.BlockSpec((None, tq, W), lambda b, r, i: (b, i, r))
    prv = pl.BlockSpec((None, WINDOW_KEYS, W), lambda b, r, i: (b, jnp.maximum(i * per - 1, 0), r))
    o, lse = pl.pallas_call(
        functools.partial(_band_body, tq=tq),
        out_shape=[jax.ShapeDtypeStruct((B, n, dil * W), BF16),
                   jax.ShapeDtypeStruct((B, dil, n, HEADS), F32)],
        grid=(B, dil, n // tq),
        in_specs=[cur, cur, cur, prv, prv],
        out_specs=[cur, pl.BlockSpec((None, None, tq, HEADS), lambda b, r, i: (b, r, i, 0))],
        compiler_params=_params(("parallel", "parallel", "arbitrary")), name="band_attention",
    )(view(q), view(k), view(v), view(k), view(v))
    return o.reshape(B, L, W), lse.transpose(0, 2, 1, 3).reshape(B, L, HEADS)


def _cross_body(q_ref, mk_ref, mv_ref, o_ref):
    for h in range(HEADS):
        hd = slice(h * HEAD_DIM, (h + 1) * HEAD_DIM)
        s = _mm_nt(q_ref[:, hd], mk_ref[:, hd].astype(BF16))
        m = jnp.max(s, axis=-1, keepdims=True)
        p = jnp.exp(s - m)
        den = jnp.sum(p, axis=-1, keepdims=True)
        o_ref[:, hd] = (_mm(p.astype(BF16), mv_ref[:, hd].astype(BF16)) / den).astype(o_ref.dtype)


def _cross_attention(q, mk, mv, *, tq):
    B, L, W = q.shape
    M = mk.shape[1]
    tok = pl.BlockSpec((None, tq, W), lambda b, i: (b, i, 0))
    mem = pl.BlockSpec((None, M, W), lambda b, i: (b, 0, 0))
    return pl.pallas_call(
        _cross_body, out_shape=jax.ShapeDtypeStruct((B, L, W), BF16),
        grid=(B, L // tq), in_specs=[tok, mem, mem], out_specs=tok,
        compiler_params=_params(("parallel", "parallel")), name="cross_attention",
    )(q, mk, mv)


def _decode_body(q_ref, kn_ref, vn_ref, ck_ref, cv_ref, o_ref, lse_ref, ok_ref, ov_ref, *,
                 window, dil, nnew):
    rows = window * HEADS
    shift = nnew * HEADS
    ok_ref[0:rows - shift, :] = ck_ref[shift:rows, :]
    ov_ref[0:rows - shift, :] = cv_ref[shift:rows, :]
    ok_ref[rows - shift:rows, :] = kn_ref[...]
    ov_ref[rows - shift:rows, :] = vn_ref[...]
    key_idx = lax.broadcasted_iota(jnp.int32, (WINDOW_KEYS, 1), 0)
    new_idx = lax.broadcasted_iota(jnp.int32, (nnew, 1), 0)
    for i in range(nnew):
        for h in range(HEADS):
            r = i * HEADS + h
            qv = q_ref[r:r + 1, :]
            start = h if dil == 1 else r
            sel = pl.ds(start, WINDOW_KEYS, stride=HEADS * dil)
            s = jnp.sum(ck_ref[sel, :] * qv, axis=-1, keepdims=True)
            if dil == 1:
                s = jnp.where(key_idx >= i, s, NEG)
                new_ok = new_idx <= i
            else:
                new_ok = new_idx == i
            new_sel = pl.ds(h, nnew, stride=HEADS)
            sn = jnp.where(new_ok, jnp.sum(kn_ref[new_sel, :] * qv, axis=-1, keepdims=True), NEG)
            m = jnp.maximum(jnp.max(s, axis=0, keepdims=True), jnp.max(sn, axis=0, keepdims=True))
            p = jnp.exp(s - m)
            pn = jnp.exp(sn - m)
            den = jnp.sum(p, axis=0, keepdims=True) + jnp.sum(pn, axis=0, keepdims=True)
            acc = (jnp.sum(p * cv_ref[sel, :], axis=0, keepdims=True)
                   + jnp.sum(pn * vn_ref[new_sel, :], axis=0, keepdims=True))
            o_ref[r:r + 1, :] = acc / den
            lse_ref[r:r + 1, :] = jnp.broadcast_to(m + jnp.log(den), (1, LANES))


def _decode_attention(q, kn, vn, ck, cv, window, dil):
    B, nrow, _ = q.shape
    rows = window * HEADS
    small = pl.BlockSpec((None, nrow, LANES), lambda b: (b, 0, 0))
    big = pl.BlockSpec((None, rows, LANES), lambda b: (b, 0, 0))
    return pl.pallas_call(
        functools.partial(_decode_body, window=window, dil=dil, nnew=nrow // HEADS),
        out_shape=[jax.ShapeDtypeStruct((B, nrow, LANES), F32),
                   jax.ShapeDtypeStruct((B, nrow, LANES), F32),
                   jax.ShapeDtypeStruct((B, rows, LANES), F32),
                   jax.ShapeDtypeStruct((B, rows, LANES), F32)],
        grid=(B,), in_specs=[small, small, small, big, big],
        out_specs=[small, small, big, big],
        compiler_params=_params(("parallel",)), name="decode_attention",
    )(q, kn, vn, ck, cv)


def _merge_body(x_ref, a_ref, o1_ref, o2_ref, o3_ref, l1_ref, l2_ref, l3_ref, ox_ref, g_ref,
                wa_ref, wb_ref, wc_ref, wo_ref, h_ref):
    l1, l2, l3 = l1_ref[...], l2_ref[...], l3_ref[...]
    lmax = jnp.maximum(jnp.maximum(l1, l2), l3)
    e1, e2, e3 = jnp.exp(l1 - lmax), jnp.exp(l2 - lmax), jnp.exp(l3 - lmax)
    inv = 1.0 / (e1 + e2 + e3)
    heads = []
    for h in range(HEADS):
        hd = slice(h * HEAD_DIM, (h + 1) * HEAD_DIM)
        heads.append((e1[:, h:h + 1] * inv[:, h:h + 1]) * o1_ref[:, hd].astype(F32)
                     + (e2[:, h:h + 1] * inv[:, h:h + 1]) * o2_ref[:, hd].astype(F32)
                     + (e3[:, h:h + 1] * inv[:, h:h + 1]) * o3_ref[:, hd].astype(F32))
    o_dil = jnp.concatenate(heads, axis=1).astype(BF16)
    ya = _mm(a_ref[...], wa_ref[...])
    yb = _mm(o_dil, wb_ref[...])
    yc = _mm(ox_ref[...], wc_ref[...])
    d = D_MODEL
    mix = (g_ref[:, 0:d].astype(F32) * ya + g_ref[:, d:2 * d].astype(F32) * yb
           + g_ref[:, 2 * d:3 * d].astype(F32) * yc)
    h_ref[...] = x_ref[...] + _mm(mix.astype(BF16), wo_ref[...])


def _merge(x, a, o_groups, lse_groups, ox, gates, wa, wb, wc, wo, *, tm):
    T = x.shape[0]
    tok = lambda w: pl.BlockSpec((tm, w), lambda i: (i, 0))
    full = lambda w: pl.BlockSpec(w.shape, lambda i: (0, 0))
    return pl.pallas_call(
        _merge_body, out_shape=jax.ShapeDtypeStruct((T, D_MODEL), F32), grid=(T // tm,),
        in_specs=[tok(D_MODEL), tok(HEADS * GLA_DV), tok(512), tok(512), tok(512),
                  tok(HEADS), tok(HEADS), tok(HEADS), tok(512), tok(3 * D_MODEL),
                  full(wa), full(wb), full(wc), full(wo)],
        out_specs=tok(D_MODEL),
        compiler_params=_params(("parallel",)), name="merge",
    )(x, a, *o_groups, *lse_groups, ox, gates, wa, wb, wc, wo)


def _ffn_body(h_ref, gn_ref, wu_ref, wd_ref, gf_ref, y_ref):
    h = h_ref[...]
    n = _rms(h, gn_ref[...]).astype(BF16)
    acc = h
    for c0 in range(0, D_FF, COL_CHUNK):
        u = jnp.maximum(_mm(n, wu_ref[:, c0:c0 + COL_CHUNK]), 0.0)
        acc = acc + _mm((u * u).astype(BF16), wd_ref[c0:c0 + COL_CHUNK, :])
    y_ref[...] = _rms(acc, gf_ref[...])


def _ffn(h, gain_ffn, wu, wd, gain_final, *, tm):
    T = h.shape[0]
    tok = pl.BlockSpec((tm, D_MODEL), lambda i: (i, 0))
    full = lambda w: pl.BlockSpec(w.shape, lambda i: (0, 0))
    return pl.pallas_call(
        _ffn_body, out_shape=jax.ShapeDtypeStruct((T, D_MODEL), F32), grid=(T // tm,),
        in_specs=[tok, full(gain_ffn), full(wu), full(wd), full(gain_final)], out_specs=tok,
        compiler_params=_params(("parallel",)), name="ffn",
    )(h, gain_ffn, wu, wd, gain_final)


def _rope_tables(pos):
    half = HEAD_DIM // 2
    inv = ROPE_THETA ** (-jnp.arange(half, dtype=F32) / half)
    ang = pos.astype(F32)[:, None] * inv[None, :]
    cos, sin = jnp.cos(ang), jnp.sin(ang)
    return jnp.concatenate([cos, cos], axis=-1), jnp.concatenate([-sin, sin], axis=-1)


def _split_weights(w_in, w_decay):
    hw = HEADS * HEAD_DIM
    sizes = [hw, hw, HEADS * GLA_DV, HEADS * GLA_DV, GLA_RANK] + [hw] * 9 + [hw, 3 * D_MODEL]
    offs = [0]
    for s in sizes:
        offs.append(offs[-1] + s)
    cols = [w_in[:, offs[j]:offs[j + 1]].astype(BF16) for j in range(len(sizes))]
    cols[4] = jnp.pad(cols[4], ((0, 0), (0, LANES - GLA_RANK)))
    wdec = jnp.pad(w_decay.astype(BF16), ((0, LANES - GLA_RANK), (0, 0)))
    return cols, wdec


def _layer(x, pos, tail_rows, cols, wdec, b_decay, b_gate, norm_mix, tm):
    cos, sin = _rope_tables(pos)
    gain = norm_mix[None, :]
    gq, gk, gv, gr, lf = _project(
        x, gain, cos, sin,
        [("lin", cols[0], QK_SCALE, BF16, 0, False), ("lin", cols[1], 1.0, BF16, 0, False),
         ("lin", cols[2], 1.0, BF16, 0, False), ("lin", cols[3], 1.0, BF16, 0, False),
         ("decay", cols[4], 1.0, F32, 0, False)],
        tm=tm, wdec=wdec, bdec=b_decay[None, :])
    dil_segs = []
    for g in range(3):
        dil_segs += [("rope", cols[5 + 3 * g], QK_SCALE, BF16, 0, False),
                     ("rope", cols[6 + 3 * g], 1.0, BF16, 0, True),
                     ("lin", cols[7 + 3 * g], 1.0, BF16, 0, True)]
    dil_out = _project(x, gain, cos, sin, dil_segs, tm=tm, tail_rows=tail_rows)
    xq, gates = _project(
        x, gain, cos, sin,
        [("lin", cols[14], QK_SCALE, BF16, 0, False), ("sig", cols[15], 1.0, BF16, 0, False)],
        tm=tm, bias=b_gate[None, :])
    dil = [dil_out[5 * g:5 * g + 5] for g in range(3)]
    return (gq, gk, gv, gr, lf), dil, xq, gates


def _finish(x2d, a, o_groups, lse_groups, ox, gates, post, tm):
    wa, wb, wc, wo, gain_ffn, wu, wd, gain_final = post
    T = x2d.shape[0]
    flat = lambda t: t.reshape(T, t.shape[-1])
    h = _merge(x2d, flat(a), [flat(t) for t in o_groups], [flat(t) for t in lse_groups],
               flat(ox), flat(gates), wa, wb, wc, wo, tm=tm)
    return _ffn(h, gain_ffn, wu, wd, gain_final, tm=tm)


def kernel(x_prompt, x_sample, mem_prompt, state_gla, cache_dil1_k, cache_dil1_v, cache_dil2_k, cache_dil2_v, cache_dil3_k, cache_dil3_v, cache_mem_k, cache_mem_v, norm_mix, w_in, b_gate, w_decay, b_decay, gla_head_norm, w_proj_gla, w_proj_dil, w_proj_x, norm_mem, w_mem_kv, w_out, norm_ffn, w_ffn_up, w_ffn_down, norm_final):
    B, L, D = x_prompt.shape
    SB, SL, _ = x_sample.shape
    depth = w_in.shape[0]
    assert depth == 1, "single trunk layer"
    i = 0
    cols, wdec = _split_weights(w_in[i], w_decay[i])
    post = (w_proj_gla[i].astype(BF16), w_proj_dil[i].astype(BF16), w_proj_x[i].astype(BF16),
            w_out[i].astype(BF16), norm_ffn[i][None, :], w_ffn_up[i].astype(BF16),
            w_ffn_down[i].astype(BF16), norm_final[None, :])
    head_norm = gla_head_norm[i][None, :]
    hw = HEADS * HEAD_DIM
    max_window = DIL_PATTERNS[-1][0]

    (gq, gk, gv, gr, lf), dil, xq, gates = _layer(
        x_prompt, jnp.arange(L), max_window, cols, wdec, b_decay[i], b_gate[i], norm_mix[i], 512)
    a_p, state_p = _gla(gq, gk, gv, lf, gr, jnp.zeros((B, HEADS, HEAD_DIM, GLA_DV), F32),
                        head_norm, chunk=128, sub=16, nb=1)
    o_groups, lse_groups, bufs_p = [], [], []
    for (win, dl), (dq, dk, dk_tail, dv, dv_tail) in zip(DIL_PATTERNS, dil):
        o_g, lse_g = _band_attention(dq, dk, dv, dl, tq=256)
        o_groups.append(o_g)
        lse_groups.append(lse_g)
        keep = min(win, L)
        bufs_p += [dk_tail[:, max_window - keep:].reshape(1, B, keep, HEADS, HEAD_DIM),
                   dv_tail[:, max_window - keep:].reshape(1, B, keep, HEADS, HEAD_DIM)]
    zeros_tab = jnp.zeros((N_MEM, LANES), F32)
    w_mem = w_mem_kv[i].astype(BF16)
    mk, mv = _project(
        mem_prompt, norm_mem[i][None, :], zeros_tab, zeros_tab,
        [("lin", w_mem[:, :hw], 1.0, F32, 0, False), ("lin", w_mem[:, hw:], 1.0, F32, 0, False)],
        tm=N_MEM)
    ox = _cross_attention(xq, mk, mv, tq=512)
    y_prompt = _finish(x_prompt.reshape(B * L, D), a_p, o_groups, lse_groups, ox, gates, post,
                       512).reshape(B, L, D)

    T = SB * SL
    pos_s = jnp.tile(PAST_LEN + jnp.arange(SL), SB)
    (gq, gk, gv, gr, lf), dil, xq, gates = _layer(
        x_sample.reshape(1, T, D), pos_s, T, cols, wdec, b_decay[i], b_gate[i], norm_mix[i], T)
    pad_rows = 16 - SL
    per_batch = lambda t: jnp.pad(t.reshape(SB, SL, t.shape[-1]), ((0, 0), (0, pad_rows), (0, 0)))
    a_s, state_s = _gla(per_batch(gq), per_batch(gk), per_batch(gv), per_batch(lf), per_batch(gr),
                        state_gla[i], head_norm, chunk=16, sub=16, nb=8)
    a_s = a_s[:, :SL]
    caches = ((cache_dil1_k[i], cache_dil1_v[i]), (cache_dil2_k[i], cache_dil2_v[i]),
              (cache_dil3_k[i], cache_dil3_v[i]))
    o_groups, lse_groups, bufs_s = [], [], []
    rows = lambda t: t.astype(F32).reshape(SB, SL * HEADS, HEAD_DIM)
    for (win, dl), (dq, dk, dk_tail, dv, dv_tail), (ck, cv) in zip(DIL_PATTERNS, dil, caches):
        o_g, lse_g, new_k, new_v = _decode_attention(
            rows(dq), rows(dk_tail), rows(dv_tail),
            ck.reshape(SB, win * HEADS, HEAD_DIM), cv.reshape(SB, win * HEADS, HEAD_DIM), win, dl)
        o_groups.append(o_g.reshape(SB, SL, hw).astype(BF16))
        lse_groups.append(lse_g[:, :, 0].reshape(SB, SL, HEADS))
        bufs_s += [new_k.reshape(1, SB, win, HEADS, HEAD_DIM), new_v.reshape(1, SB, win, HEADS, HEAD_DIM)]
    ox = _cross_attention(per_batch(xq), cache_mem_k[i].reshape(SB, N_MEM, hw),
                          cache_mem_v[i].reshape(SB, N_MEM, hw), tq=16)[:, :SL]
    y_sample = _finish(x_sample.reshape(T, D), a_s, o_groups, lse_groups, ox, gates, post,
                       T).reshape(SB, SL, D)

    return (y_prompt, y_sample, state_p[None], *bufs_p,
            mk.reshape(1, B, N_MEM, HEADS, HEAD_DIM), mv.reshape(1, B, N_MEM, HEADS, HEAD_DIM),
            state_s[None], *bufs_s)
```

```python
import functools

import jax
import jax.numpy as jnp
from jax import lax
from jax.experimental import pallas as pl
from jax.experimental.pallas import tpu as pltpu

F32 = jnp.float32
BF16 = jnp.bfloat16

D_MODEL = 1024
PAST_LEN = 8192
N_MEM = 256
HEADS = 4
HEAD_DIM = 128
GLA_DV = 256
GLA_RANK = 16
GLA_GATE_NORM = 16.0
DIL_PATTERNS = ((128, 1), (512, 4), (2048, 16))
WINDOW_KEYS = 128
D_FF = 4 * D_MODEL
ROPE_THETA = 10000.0
EPS = 1e-6
NEG = -1e30
QK_SCALE = HEAD_DIM ** -0.5
LOG2_E = 1.4426950408889634

LANES = 128
COL_CHUNK = 512
VMEM_LIMIT = 48 * 1024 * 1024


def _mm(a, b):
    return jnp.dot(a, b, preferred_element_type=F32)


def _mm_nt(a, b):
    return lax.dot_general(a, b, (((1,), (1,)), ((), ())), preferred_element_type=F32)


def _mm_tn(a, b):
    return lax.dot_general(a, b, (((0,), (0,)), ((), ())), preferred_element_type=F32)


def _rms(x, gain):
    return x * lax.rsqrt(jnp.mean(x * x, axis=-1, keepdims=True) + EPS) * gain


def _params(sem):
    return pltpu.CompilerParams(dimension_semantics=sem, vmem_limit_bytes=VMEM_LIMIT)


def _lane_blocks(y):
    return [y[:, j * LANES:(j + 1) * LANES] for j in range(y.shape[1] // LANES)]


def _residue_major(ref, dil):
    nblk, tm, _ = ref.shape
    per = tm // dil
    return jnp.concatenate(
        [jnp.concatenate([ref[j, pl.ds(r, per, stride=dil), :] for j in range(nblk)], axis=1)
         for r in range(dil)], axis=0)


def _store_token_order(buf, y, dil):
    per = y.shape[0] // dil
    for r in range(dil):
        for j, blk in enumerate(_lane_blocks(y[r * per:(r + 1) * per])):
            buf[j, pl.ds(r, per, stride=dil), :] = blk


def _proj_body(*refs, plan, first_tail_tile, dils):
    x_ref, g_ref, cos_ref, sin_ref, b_ref, wdec_ref, bdec_ref = refs[:7]
    w_refs = refs[7:7 + len(plan)]
    nout = sum(2 if p[4] else 1 for p in plan)
    out_refs = list(refs[7 + len(plan):7 + len(plan) + nout])
    scratch = refs[7 + len(plan) + nout:]
    tm = x_ref.shape[0]
    n32 = _rms(x_ref[...], g_ref[...])
    if scratch:
        n_scr, tail_scr = scratch
        for j, blk in enumerate(_lane_blocks(n32)):
            n_scr[j] = blk
    rows_of = {1: n32.astype(BF16)}
    tables = {1: (cos_ref[...], sin_ref[...])}
    for d in dils:
        if d > 1:
            rows_of[d] = _residue_major(n_scr, d).astype(BF16)
            tables[d] = tuple(
                jnp.concatenate([t[pl.ds(r, tm // d, stride=d), :] for r in range(d)], axis=0)
                for t in (cos_ref, sin_ref))
    in_tail = pl.program_id(1) >= first_tail_tile
    oi = 0
    for (kind, scale, width, boff, has_tail, dil), w_ref in zip(plan, w_refs):
        out = out_refs[oi]
        oi += 1
        tail = None
        if has_tail:
            tail = out_refs[oi]
            oi += 1
        n = rows_of[dil]
        if kind == "decay":
            ga = _mm(n, w_ref[...]).astype(BF16)
            z = _mm(ga, wdec_ref[...]) + bdec_ref[...]
            logsig = jnp.minimum(z, 0.0) - jnp.log1p(jnp.exp(-jnp.abs(z)))
            out[...] = logsig * (1.0 / GLA_GATE_NORM)
            continue
        per = tm // dil
        for c0 in range(0, width, COL_CHUNK):
            cw = min(COL_CHUNK, width - c0)
            y = _mm(n, w_ref[:, c0:c0 + cw])
            if kind == "rope":
                cosv, sinv = tables[dil]
                heads = []
                for j in range(cw // HEAD_DIM):
                    yh = y[:, j * HEAD_DIM:(j + 1) * HEAD_DIM]
                    heads.append(yh * cosv + pltpu.roll(yh, HEAD_DIM // 2, 1) * sinv)
                y = jnp.concatenate(heads, axis=1)
            elif kind == "sig":
                y = jax.nn.sigmoid(y + b_ref[:, boff + c0:boff + c0 + cw])
            if tail is not None:
                @pl.when(in_tail)
                def _():
                    if dil == 1:
                        tail[:, c0:c0 + cw] = y
                    else:
                        _store_token_order(tail_scr, y, dil)
                        for j in range(cw // LANES):
                            tail[:, c0 + j * LANES:c0 + (j + 1) * LANES] = tail_scr[j]
            if scale != 1.0:
                y = y * scale
            for r in range(dil):
                out[:, r * width + c0:r * width + c0 + cw] = y[r * per:(r + 1) * per].astype(out.dtype)


def _project(x, gain, cos, sin, segs, *, tm, tail_rows=0, bias=None, wdec=None, bdec=None):
    B, L, D = x.shape
    nL = L // tm
    ntail = tail_rows // tm
    first_tail_tile = nL - ntail
    if bias is None:
        bias = jnp.zeros((1, LANES), F32)
    if wdec is None:
        wdec = jnp.zeros((LANES, LANES), BF16)
        bdec = jnp.zeros((1, LANES), F32)

    def const2(shape):
        return pl.BlockSpec(shape, lambda b, i: (0, 0))

    in_specs = [
        pl.BlockSpec((None, tm, D), lambda b, i: (b, i, 0)),
        const2((1, D)),
        pl.BlockSpec((tm, LANES), lambda b, i: (i, 0)),
        pl.BlockSpec((tm, LANES), lambda b, i: (i, 0)),
        const2(bias.shape), const2(wdec.shape), const2(bdec.shape),
    ]
    plan, weights, out_shapes, out_specs = [], [], [], []
    for kind, w, scale, dtype, boff, has_tail, dil in segs:
        width = w.shape[1]
        owidth = wdec.shape[1] if kind == "decay" else width
        plan.append((kind, scale, width, boff, has_tail, dil))
        weights.append(w)
        in_specs.append(const2(w.shape))
        out_shapes.append(jax.ShapeDtypeStruct((B, L // dil, dil * owidth), dtype))
        out_specs.append(pl.BlockSpec((None, tm // dil, dil * owidth), lambda b, i: (b, i, 0)))
        if has_tail:
            out_shapes.append(jax.ShapeDtypeStruct((B, tail_rows, width), F32))
            out_specs.append(pl.BlockSpec(
                (None, tm, width),
                lambda b, i: (b, jnp.maximum(i - first_tail_tile, 0), 0)))
    dils = tuple(sorted({p[5] for p in plan}))
    scratch = []
    if dils != (1,):
        scratch = [pltpu.VMEM((D // LANES, tm, LANES), F32),
                   pltpu.VMEM((COL_CHUNK // LANES, tm, LANES), F32)]
    body = functools.partial(_proj_body, plan=tuple(plan), first_tail_tile=first_tail_tile,
                             dils=dils)
    return pl.pallas_call(
        body, out_shape=out_shapes, grid=(B, nL), in_specs=in_specs, out_specs=out_specs,
        scratch_shapes=scratch,
        compiler_params=_params(("parallel", "arbitrary")), name="project",
    )(x, gain, cos, sin, bias, wdec, bdec, *weights)


GLA_SUB = 8


def _gla_level_masks(T):
    row = lax.broadcasted_iota(jnp.int32, (T, T), 0)
    col = lax.broadcasted_iota(jnp.int32, (T, T), 1)
    rid = lax.broadcasted_iota(jnp.int32, (T, 1), 0)
    levels = []
    bs = 2 * GLA_SUB
    while bs <= T:
        half = bs // 2
        second = (rid & (bs - 1)) >= half
        pair = ((row & -bs) == (col & -bs)) & ((row & (bs - 1)) >= half) & ((col & (bs - 1)) < half)
        levels.append((second, pair, bs))
        bs *= 2
    return levels


def _gla_matmuls(qf, kf, v, cum, s0, levels):
    T = qf.shape[0]
    last = cum[T - 1:T, :]
    o_inter = _mm((qf * jnp.exp2(cum)).astype(BF16), s0.astype(BF16))
    off = jnp.zeros((T, T), F32)
    for second, pair_mask, bs in levels:
        bound = []
        for j in range(T // bs):
            mid_row = j * bs + bs // 2 - 1
            mid = jnp.broadcast_to(cum[mid_row:mid_row + 1, :], (GLA_SUB, HEAD_DIM))
            bound += [mid] * (bs // GLA_SUB)
        rel = cum - jnp.concatenate(bound, axis=0)
        e = jnp.exp2(jnp.where(second, rel, -rel))
        off = jnp.where(pair_mask, _mm_nt((qf * e).astype(BF16), (kf * e).astype(BF16)), off)
    upd = _mm_tn((kf * jnp.exp2(last - cum)).astype(BF16), v)
    er = lax.broadcasted_iota(jnp.int32, (HEAD_DIM, HEAD_DIM), 0)
    ec = lax.broadcasted_iota(jnp.int32, (HEAD_DIM, HEAD_DIM), 1)
    decay_col = jnp.sum(jnp.where(er == ec, jnp.exp2(last), 0.0), axis=1, keepdims=True)
    return o_inter, off, decay_col * s0 + upd


def _gla_diagonal(qf, kf, cum, off, diag_masks):
    T = qf.shape[0]
    tiles = (T // GLA_SUB, GLA_SUB, HEAD_DIM)
    q3, k3, c3 = qf.reshape(tiles), kf.reshape(tiles), cum.reshape(tiles)
    att3 = off.reshape(T // GLA_SUB, GLA_SUB, T)
    for j in range(GLA_SUB):
        if j == 0:
            prod = q3 * k3
        else:
            prod = q3 * pltpu.roll(k3, j, 1) * jnp.exp2(c3 - pltpu.roll(c3, j, 1))
        att3 = jnp.where(diag_masks[j], jnp.sum(prod, axis=-1, keepdims=True), att3)
    return att3.reshape(T, T)


def _gla_body(q_ref, k_ref, v_ref, lf_ref, r_ref, s0_ref, hn_ref, a_ref, s_ref, *, nb):
    T = q_ref.shape[1]
    hw = HEADS * HEAD_DIM
    first = pl.program_id(1) == 0
    row = lax.broadcasted_iota(jnp.int32, (T, T), 0)
    col = lax.broadcasted_iota(jnp.int32, (T, T), 1)
    tri = jnp.where(row >= col, 1.0, 0.0).astype(BF16)
    levels = _gla_level_masks(T)
    shape3 = (T // GLA_SUB, GLA_SUB, T)
    sub_row = lax.broadcasted_iota(jnp.int32, shape3, 1)
    token = lax.broadcasted_iota(jnp.int32, shape3, 0) * GLA_SUB + sub_row
    key = lax.broadcasted_iota(jnp.int32, shape3, 2)
    diag_masks = [(key == token - j) & (sub_row >= j) for j in range(GLA_SUB)]

    @pl.when(first)
    def _():
        s_ref[...] = s0_ref[...]

    cums = []
    for b in range(nb):
        lf = lf_ref[b] * LOG2_E
        hi = lf.astype(BF16)
        lo = (lf - hi.astype(F32)).astype(BF16)
        parts = _mm(tri, jnp.concatenate([hi, lo], axis=1))
        cums.append(parts[:, :hw] + parts[:, hw:])
    pairs = [(b, h) for b in range(nb) for h in range(HEADS)]
    dk = lambda h: slice(h * HEAD_DIM, (h + 1) * HEAD_DIM)
    dv = lambda h: slice(h * GLA_DV, (h + 1) * GLA_DV)
    qk = {(b, h): (q_ref[b, :, dk(h)].astype(F32), k_ref[b, :, dk(h)].astype(F32), cums[b][:, dk(h)])
          for b, h in pairs}
    stage1 = {}
    for b, h in pairs:
        qf, kf, cum = qk[b, h]
        stage1[b, h] = _gla_matmuls(qf, kf, v_ref[b, :, dv(h)], cum, s_ref[b, h], levels)
        s_ref[b, h] = stage1[b, h][2]
    att = {}
    for b, h in pairs:
        qf, kf, cum = qk[b, h]
        att[b, h] = _gla_diagonal(qf, kf, cum, stage1[b, h][1], diag_masks).astype(BF16)
    for b, h in pairs:
        o = stage1[b, h][0] + _mm(att[b, h], v_ref[b, :, dv(h)])
        r = r_ref[b, :, dv(h)].astype(F32)
        a_ref[b, :, dv(h)] = (_rms(o, hn_ref[...]) * (r * jax.nn.sigmoid(r))).astype(a_ref.dtype)


def _gla(q, k, v, lf, r, s0, head_norm, *, chunk, nb):
    B, L, hw = q.shape
    tok = lambda w: pl.BlockSpec((nb, chunk, w), lambda b, c: (b, c, 0))
    state = pl.BlockSpec((nb, HEADS, HEAD_DIM, GLA_DV), lambda b, c: (b, 0, 0, 0))
    return pl.pallas_call(
        functools.partial(_gla_body, nb=nb),
        out_shape=[jax.ShapeDtypeStruct((B, L, HEADS * GLA_DV), BF16),
                   jax.ShapeDtypeStruct(s0.shape, F32)],
        grid=(B // nb, L // chunk),
        in_specs=[tok(512), tok(512), tok(1024), tok(512), tok(1024), state,
                  pl.BlockSpec((1, GLA_DV), lambda b, c: (0, 0))],
        out_specs=[tok(1024), state],
        compiler_params=_params(("parallel", "arbitrary")), name="gla",
    )(q, k, v, lf, r, s0, head_norm)


def _band_body(q_ref, k_ref, v_ref, kp_ref, vp_ref, o_ref, lse_ref, *, tq):
    step = pl.program_id(2)
    nblk = tq // WINDOW_KEYS
    t = lax.broadcasted_iota(jnp.int32, (WINDOW_KEYS, 2 * WINDOW_KEYS), 0)
    c = lax.broadcasted_iota(jnp.int32, (WINDOW_KEYS, 2 * WINDOW_KEYS), 1)
    band = (c >= t) & (c <= t + WINDOW_KEYS)
    lane = lax.broadcasted_iota(jnp.int32, (WINDOW_KEYS, LANES), 1)
    rows = lambda j: slice(j * WINDOW_KEYS, (j + 1) * WINDOW_KEYS)
    hd = lambda h: slice(h * HEAD_DIM, (h + 1) * HEAD_DIM)

    def window(cur_ref, prev_ref, j, h):
        before = prev_ref[:, hd(h)] if j == 0 else cur_ref[rows(j - 1), hd(h)]
        return jnp.concatenate([before, cur_ref[rows(j), hd(h)]], axis=0)

    pairs = [(j, h) for j in range(nblk) for h in range(HEADS)]
    scores = [_mm_nt(q_ref[rows(j), hd(h)], window(k_ref, kp_ref, j, h)) for j, h in pairs]
    probs = []
    for (j, h), s in zip(pairs, scores):
        first_key = step * tq + (j - 1) * WINDOW_KEYS
        s = jnp.where(band & (c + first_key >= 0), s, NEG)
        m = jnp.max(s, axis=-1, keepdims=True)
        p = jnp.exp(s - m)
        den = jnp.sum(p, axis=-1, keepdims=True)
        probs.append((p.astype(BF16), den, m + jnp.log(den)))
    lse_blk = [jnp.zeros((WINDOW_KEYS, LANES), F32) for _ in range(nblk)]
    for (j, h), (p, den, lse) in zip(pairs, probs):
        o_ref[rows(j), hd(h)] = (_mm(p, window(v_ref, vp_ref, j, h)) / den).astype(o_ref.dtype)
        lse_blk[j] = jnp.where(lane == h, lse, lse_blk[j])
    for j in range(nblk):
        lse_ref[rows(j), :] = lse_blk[j][:, :HEADS]


def _band_attention(q, k, v, dil, *, tq):
    B, n, _ = q.shape
    W = HEADS * HEAD_DIM
    per = tq // WINDOW_KEYS
    cur = pl.BlockSpec((None, tq, W), lambda b, r, i: (b, i, r))
    prv = pl.BlockSpec((None, WINDOW_KEYS, W), lambda b, r, i: (b, jnp.maximum(i * per - 1, 0), r))
    o, lse = pl.pallas_call(
        functools.partial(_band_body, tq=tq),
        out_shape=[jax.ShapeDtypeStruct((B, n, dil * W), BF16),
                   jax.ShapeDtypeStruct((B, dil, n, HEADS), F32)],
        grid=(B, dil, n // tq),
        in_specs=[cur, cur, cur, prv, prv],
        out_specs=[cur, pl.BlockSpec((None, None, tq, HEADS), lambda b, r, i: (b, r, i, 0))],
        compiler_params=_params(("parallel", "parallel", "arbitrary")), name="band_attention",
    )(q, k, v, k, v)
    return o, lse.transpose(0, 2, 1, 3).reshape(B, n * dil, HEADS)


def _cross_body(q_ref, mk_ref, mv_ref, o_ref, *, nb):
    slots = mk_ref.shape[1] // HEADS
    pairs = [(b, h) for b in range(nb) for h in range(HEADS)]
    head_rows = lambda h: pl.ds(h, slots, stride=HEADS)
    lanes = lambda h: slice(h * HEAD_DIM, (h + 1) * HEAD_DIM)
    scores = [_mm_nt(q_ref[b, :, lanes(h)], mk_ref[b, head_rows(h), :].astype(BF16))
              for b, h in pairs]
    probs = []
    for s in scores:
        p = jnp.exp(s - jnp.max(s, axis=-1, keepdims=True))
        probs.append((p.astype(BF16), jnp.sum(p, axis=-1, keepdims=True)))
    for (b, h), (p, den) in zip(pairs, probs):
        pv = _mm(p, mv_ref[b, head_rows(h), :].astype(BF16))
        o_ref[b, :, lanes(h)] = (pv / den).astype(o_ref.dtype)


def _cross_attention(q, mk, mv, *, tq, nb):
    B, L, W = q.shape
    tok = pl.BlockSpec((nb, tq, W), lambda b, i: (b, i, 0))
    mem = pl.BlockSpec((nb,) + mk.shape[1:], lambda b, i: (b, 0, 0))
    return pl.pallas_call(
        functools.partial(_cross_body, nb=nb), out_shape=jax.ShapeDtypeStruct((B, L, W), BF16),
        grid=(B // nb, L // tq), in_specs=[tok, mem, mem], out_specs=tok,
        compiler_params=_params(("parallel", "parallel")), name="cross_attention",
    )(q, mk, mv)


def _decode_body(q_ref, kn_ref, vn_ref, ck_ref, cv_ref, o_ref, lse_ref, ok_ref, ov_ref, *,
                 window, dil, nnew):
    rows = window * HEADS
    shift = nnew * HEADS
    ok_ref[0:rows - shift, :] = ck_ref[shift:rows, :]
    ov_ref[0:rows - shift, :] = cv_ref[shift:rows, :]
    ok_ref[rows - shift:rows, :] = kn_ref[...]
    ov_ref[rows - shift:rows, :] = vn_ref[...]
    key_idx = lax.broadcasted_iota(jnp.int32, (WINDOW_KEYS, 1), 0)
    new_idx = lax.broadcasted_iota(jnp.int32, (nnew, 1), 0)
    for i in range(nnew):
        for h in range(HEADS):
            r = i * HEADS + h
            qv = q_ref[r:r + 1, :]
            start = h if dil == 1 else r
            sel = pl.ds(start, WINDOW_KEYS, stride=HEADS * dil)
            s = jnp.sum(ck_ref[sel, :] * qv, axis=-1, keepdims=True)
            if dil == 1:
                s = jnp.where(key_idx >= i, s, NEG)
                new_ok = new_idx <= i
            else:
                new_ok = new_idx == i
            new_sel = pl.ds(h, nnew, stride=HEADS)
            sn = jnp.where(new_ok, jnp.sum(kn_ref[new_sel, :] * qv, axis=-1, keepdims=True), NEG)
            m = jnp.maximum(jnp.max(s, axis=0, keepdims=True), jnp.max(sn, axis=0, keepdims=True))
            p = jnp.exp(s - m)
            pn = jnp.exp(sn - m)
            den = jnp.sum(p, axis=0, keepdims=True) + jnp.sum(pn, axis=0, keepdims=True)
            acc = (jnp.sum(p * cv_ref[sel, :], axis=0, keepdims=True)
                   + jnp.sum(pn * vn_ref[new_sel, :], axis=0, keepdims=True))
            o_ref[r:r + 1, :] = acc / den
            lse_ref[r:r + 1, :] = jnp.broadcast_to(m + jnp.log(den), (1, LANES))


def _decode_attention(q, kn, vn, ck, cv, window, dil):
    B, nrow, _ = q.shape
    rows = window * HEADS
    small = pl.BlockSpec((None, nrow, LANES), lambda b: (b, 0, 0))
    big = pl.BlockSpec((None, rows, LANES), lambda b: (b, 0, 0))
    return pl.pallas_call(
        functools.partial(_decode_body, window=window, dil=dil, nnew=nrow // HEADS),
        out_shape=[jax.ShapeDtypeStruct((B, nrow, LANES), F32),
                   jax.ShapeDtypeStruct((B, nrow, LANES), F32),
                   jax.ShapeDtypeStruct((B, rows, LANES), F32),
                   jax.ShapeDtypeStruct((B, rows, LANES), F32)],
        grid=(B,), in_specs=[small, small, small, big, big],
        out_specs=[small, small, big, big],
        compiler_params=_params(("parallel",)), name="decode_attention",
    )(q, kn, vn, ck, cv)


def _merge_body(x_ref, a_ref, o1_ref, o2_ref, o3_ref, l1_ref, l2_ref, l3_ref, ox_ref, g_ref,
                wa_ref, wb_ref, wc_ref, wo_ref, h_ref, *scratch, dils):
    tm = x_ref.shape[0]
    hw = HEADS * HEAD_DIM
    groups, si = [], 0
    for o_ref, d in zip((o1_ref, o2_ref, o3_ref), dils):
        if d == 1:
            groups.append([o_ref[:, h * HEAD_DIM:(h + 1) * HEAD_DIM] for h in range(HEADS)])
            continue
        buf = scratch[si]
        si += 1
        for r in range(d):
            for h in range(HEADS):
                lanes = slice(r * hw + h * HEAD_DIM, r * hw + (h + 1) * HEAD_DIM)
                buf[h, pl.ds(r, tm // d, stride=d), :] = o_ref[:, lanes].astype(F32)
        groups.append([buf[h] for h in range(HEADS)])
    l1, l2, l3 = l1_ref[...], l2_ref[...], l3_ref[...]
    lmax = jnp.maximum(jnp.maximum(l1, l2), l3)
    e1, e2, e3 = jnp.exp(l1 - lmax), jnp.exp(l2 - lmax), jnp.exp(l3 - lmax)
    inv = 1.0 / (e1 + e2 + e3)
    heads = []
    for h in range(HEADS):
        heads.append((e1[:, h:h + 1] * inv[:, h:h + 1]) * groups[0][h].astype(F32)
                     + (e2[:, h:h + 1] * inv[:, h:h + 1]) * groups[1][h].astype(F32)
                     + (e3[:, h:h + 1] * inv[:, h:h + 1]) * groups[2][h].astype(F32))
    o_dil = jnp.concatenate(heads, axis=1).astype(BF16)
    ya = _mm(a_ref[...], wa_ref[...])
    yb = _mm(o_dil, wb_ref[...])
    yc = _mm(ox_ref[...], wc_ref[...])
    d = D_MODEL
    mix = (g_ref[:, 0:d].astype(F32) * ya + g_ref[:, d:2 * d].astype(F32) * yb
           + g_ref[:, 2 * d:3 * d].astype(F32) * yc)
    h_ref[...] = x_ref[...] + _mm(mix.astype(BF16), wo_ref[...])


def _merge(x, a, o_groups, lse_groups, ox, gates, wa, wb, wc, wo, *, tm, dils):
    T = x.shape[0]
    hw = HEADS * HEAD_DIM
    tok = lambda w: pl.BlockSpec((tm, w), lambda i: (i, 0))
    grp = lambda d: pl.BlockSpec((tm // d, d * hw), lambda i: (i, 0))
    full = lambda w: pl.BlockSpec(w.shape, lambda i: (0, 0))
    return pl.pallas_call(
        functools.partial(_merge_body, dils=dils),
        out_shape=jax.ShapeDtypeStruct((T, D_MODEL), F32), grid=(T // tm,),
        in_specs=[tok(D_MODEL), tok(HEADS * GLA_DV), grp(dils[0]), grp(dils[1]), grp(dils[2]),
                  tok(HEADS), tok(HEADS), tok(HEADS), tok(hw), tok(3 * D_MODEL),
                  full(wa), full(wb), full(wc), full(wo)],
        out_specs=tok(D_MODEL),
        scratch_shapes=[pltpu.VMEM((HEADS, tm, HEAD_DIM), F32) for d in dils if d > 1],
        compiler_params=_params(("parallel",)), name="merge",
    )(x, a, *o_groups, *lse_groups, ox, gates, wa, wb, wc, wo)


def _ffn_body(h_ref, gn_ref, wu_ref, wd_ref, gf_ref, y_ref):
    h = h_ref[...]
    n = _rms(h, gn_ref[...]).astype(BF16)
    acc = h
    for c0 in range(0, D_FF, COL_CHUNK):
        u = jnp.maximum(_mm(n, wu_ref[:, c0:c0 + COL_CHUNK]), 0.0)
        acc = acc + _mm((u * u).astype(BF16), wd_ref[c0:c0 + COL_CHUNK, :])
    y_ref[...] = _rms(acc, gf_ref[...])


def _ffn(h, gain_ffn, wu, wd, gain_final, *, tm):
    T = h.shape[0]
    tok = pl.BlockSpec((tm, D_MODEL), lambda i: (i, 0))
    full = lambda w: pl.BlockSpec(w.shape, lambda i: (0, 0))
    return pl.pallas_call(
        _ffn_body, out_shape=jax.ShapeDtypeStruct((T, D_MODEL), F32), grid=(T // tm,),
        in_specs=[tok, full(gain_ffn), full(wu), full(wd), full(gain_final)], out_specs=tok,
        compiler_params=_params(("parallel",)), name="ffn",
    )(h, gain_ffn, wu, wd, gain_final)


def _rope_tables(pos):
    half = HEAD_DIM // 2
    inv = ROPE_THETA ** (-jnp.arange(half, dtype=F32) / half)
    ang = pos.astype(F32)[:, None] * inv[None, :]
    cos, sin = jnp.cos(ang), jnp.sin(ang)
    return jnp.concatenate([cos, cos], axis=-1), jnp.concatenate([-sin, sin], axis=-1)


def _split_weights(w_in, w_decay):
    hw = HEADS * HEAD_DIM
    sizes = [hw, hw, HEADS * GLA_DV, HEADS * GLA_DV, GLA_RANK] + [hw] * 9 + [hw, 3 * D_MODEL]
    offs = [0]
    for s in sizes:
        offs.append(offs[-1] + s)
    cols = [w_in[:, offs[j]:offs[j + 1]].astype(BF16) for j in range(len(sizes))]
    cols[4] = jnp.pad(cols[4], ((0, 0), (0, LANES - GLA_RANK)))
    wdec = jnp.pad(w_decay.astype(BF16), ((0, LANES - GLA_RANK), (0, 0)))
    return cols, wdec


def _layer(x, pos, tail_rows, cols, wdec, b_decay, b_gate, norm_mix, tm, dils):
    cos, sin = _rope_tables(pos)
    gain = norm_mix[None, :]
    gq, gk, gv, gr, lf = _project(
        x, gain, cos, sin,
        [("lin", cols[0], QK_SCALE, BF16, 0, False, 1), ("lin", cols[1], 1.0, BF16, 0, False, 1),
         ("lin", cols[2], 1.0, BF16, 0, False, 1), ("lin", cols[3], 1.0, BF16, 0, False, 1),
         ("decay", cols[4], 1.0, F32, 0, False, 1)],
        tm=tm, wdec=wdec, bdec=b_decay[None, :])
    dil_segs = []
    for g in range(3):
        dil_segs += [("rope", cols[5 + 3 * g], QK_SCALE, BF16, 0, False, dils[g]),
                     ("rope", cols[6 + 3 * g], 1.0, BF16, 0, True, dils[g]),
                     ("lin", cols[7 + 3 * g], 1.0, BF16, 0, True, dils[g])]
    dil_out = _project(x, gain, cos, sin, dil_segs, tm=tm, tail_rows=tail_rows)
    xq, gates = _project(
        x, gain, cos, sin,
        [("lin", cols[14], QK_SCALE, BF16, 0, False, 1), ("sig", cols[15], 1.0, BF16, 0, False, 1)],
        tm=tm, bias=b_gate[None, :])
    dil = [dil_out[5 * g:5 * g + 5] for g in range(3)]
    return (gq, gk, gv, gr, lf), dil, xq, gates


def _finish(x2d, a, o_groups, lse_groups, ox, gates, post, tm, dils):
    wa, wb, wc, wo, gain_ffn, wu, wd, gain_final = post
    flat = lambda t: t.reshape(-1, t.shape[-1])
    h = _merge(x2d, flat(a), [flat(t) for t in o_groups], [flat(t) for t in lse_groups],
               flat(ox), flat(gates), wa, wb, wc, wo, tm=tm, dils=dils)
    return _ffn(h, gain_ffn, wu, wd, gain_final, tm=tm)


def kernel(x_prompt, x_sample, mem_prompt, state_gla, cache_dil1_k, cache_dil1_v, cache_dil2_k, cache_dil2_v, cache_dil3_k, cache_dil3_v, cache_mem_k, cache_mem_v, norm_mix, w_in, b_gate, w_decay, b_decay, gla_head_norm, w_proj_gla, w_proj_dil, w_proj_x, norm_mem, w_mem_kv, w_out, norm_ffn, w_ffn_up, w_ffn_down, norm_final):
    B, L, D = x_prompt.shape
    SB, SL, _ = x_sample.shape
    depth = w_in.shape[0]
    assert depth == 1, "single trunk layer"
    i = 0
    cols, wdec = _split_weights(w_in[i], w_decay[i])
    post = (w_proj_gla[i].astype(BF16), w_proj_dil[i].astype(BF16), w_proj_x[i].astype(BF16),
            w_out[i].astype(BF16), norm_ffn[i][None, :], w_ffn_up[i].astype(BF16),
            w_ffn_down[i].astype(BF16), norm_final[None, :])
    head_norm = gla_head_norm[i][None, :]
    hw = HEADS * HEAD_DIM
    max_window = DIL_PATTERNS[-1][0]

    dils_p = tuple(dl for _, dl in DIL_PATTERNS)
    (gq, gk, gv, gr, lf), dil, xq, gates = _layer(
        x_prompt, jnp.arange(L), max_window, cols, wdec, b_decay[i], b_gate[i], norm_mix[i], 512,
        dils_p)
    a_p, state_p = _gla(gq, gk, gv, lf, gr, jnp.zeros((B, HEADS, HEAD_DIM, GLA_DV), F32),
                        head_norm, chunk=128, nb=1)
    o_groups, lse_groups, bufs_p = [], [], []
    for (win, dl), (dq, dk, dk_tail, dv, dv_tail) in zip(DIL_PATTERNS, dil):
        o_g, lse_g = _band_attention(dq, dk, dv, dl, tq=512)
        o_groups.append(o_g)
        lse_groups.append(lse_g)
        keep = min(win, L)
        bufs_p += [dk_tail[:, max_window - keep:].reshape(1, B, keep, HEADS, HEAD_DIM),
                   dv_tail[:, max_window - keep:].reshape(1, B, keep, HEADS, HEAD_DIM)]
    zeros_tab = jnp.zeros((N_MEM, LANES), F32)
    w_mem = w_mem_kv[i].astype(BF16)
    mk, mv = _project(
        mem_prompt, norm_mem[i][None, :], zeros_tab, zeros_tab,
        [("lin", w_mem[:, :hw], 1.0, F32, 0, False, 1), ("lin", w_mem[:, hw:], 1.0, F32, 0, False, 1)],
        tm=N_MEM)
    mem_rows = lambda t: t.reshape(t.shape[0], N_MEM * HEADS, HEAD_DIM)
    ox = _cross_attention(xq, mem_rows(mk), mem_rows(mv), tq=512, nb=1)
    y_prompt = _finish(x_prompt.reshape(B * L, D), a_p, o_groups, lse_groups, ox, gates, post,
                       512, dils_p).reshape(B, L, D)

    T = SB * SL
    pos_s = jnp.tile(PAST_LEN + jnp.arange(SL), SB)
    (gq, gk, gv, gr, lf), dil, xq, gates = _layer(
        x_sample.reshape(1, T, D), pos_s, T, cols, wdec, b_decay[i], b_gate[i], norm_mix[i], T,
        (1, 1, 1))
    pad_rows = 16 - SL
    per_batch = lambda t: jnp.pad(t.reshape(SB, SL, t.shape[-1]), ((0, 0), (0, pad_rows), (0, 0)))
    a_s, state_s = _gla(per_batch(gq), per_batch(gk), per_batch(gv), per_batch(lf), per_batch(gr),
                        state_gla[i], head_norm, chunk=16, nb=8)
    a_s = a_s[:, :SL]
    caches = ((cache_dil1_k[i], cache_dil1_v[i]), (cache_dil2_k[i], cache_dil2_v[i]),
              (cache_dil3_k[i], cache_dil3_v[i]))
    o_groups, lse_groups, bufs_s = [], [], []
    rows = lambda t: t.astype(F32).reshape(SB, SL * HEADS, HEAD_DIM)
    for (win, dl), (dq, dk, dk_tail, dv, dv_tail), (ck, cv) in zip(DIL_PATTERNS, dil, caches):
        o_g, lse_g, new_k, new_v = _decode_attention(
            rows(dq), rows(dk_tail), rows(dv_tail),
            ck.reshape(SB, win * HEADS, HEAD_DIM), cv.reshape(SB, win * HEADS, HEAD_DIM), win, dl)
        o_groups.append(o_g.reshape(SB, SL, hw).astype(BF16))
        lse_groups.append(lse_g[:, :, 0].reshape(SB, SL, HEADS))
        bufs_s += [new_k.reshape(1, SB, win, HEADS, HEAD_DIM), new_v.reshape(1, SB, win, HEADS, HEAD_DIM)]
    ox = _cross_attention(per_batch(xq), mem_rows(cache_mem_k[i]), mem_rows(cache_mem_v[i]),
                          tq=16, nb=8)[:, :SL]
    y_sample = _finish(x_sample.reshape(T, D), a_s, o_groups, lse_groups, ox, gates, post,
                       T, (1, 1, 1)).reshape(SB, SL, D)

    return (y_prompt, y_sample, state_p[None], *bufs_p,
            mk.reshape(1, B, N_MEM, HEADS, HEAD_DIM), mv.reshape(1, B, N_MEM, HEADS, HEAD_DIM),
            state_s[None], *bufs_s)
```

```python
import functools

import jax
import jax.numpy as jnp
from jax import lax
from jax.experimental import pallas as pl
from jax.experimental.pallas import tpu as pltpu

F32 = jnp.float32
BF16 = jnp.bfloat16

D_MODEL = 1024
PAST_LEN = 8192
N_MEM = 256
HEADS = 4
HEAD_DIM = 128
GLA_DV = 256
GLA_RANK = 16
GLA_GATE_NORM = 16.0
DIL_PATTERNS = ((128, 1), (512, 4), (2048, 16))
WINDOW_KEYS = 128
D_FF = 4 * D_MODEL
ROPE_THETA = 10000.0
EPS = 1e-6
NEG = -1e30
QK_SCALE = HEAD_DIM ** -0.5
LOG2_E = 1.4426950408889634

LANES = 128
COL_CHUNK = 512
VMEM_LIMIT = 48 * 1024 * 1024


def _mm(a, b):
    return jnp.dot(a, b, preferred_element_type=F32)


def _mm_nt(a, b):
    return lax.dot_general(a, b, (((1,), (1,)), ((), ())), preferred_element_type=F32)


def _mm_tn(a, b):
    return lax.dot_general(a, b, (((0,), (0,)), ((), ())), preferred_element_type=F32)


def _rms(x, gain):
    return x * lax.rsqrt(jnp.mean(x * x, axis=-1, keepdims=True) + EPS) * gain


def _params(sem):
    return pltpu.CompilerParams(dimension_semantics=sem, vmem_limit_bytes=VMEM_LIMIT)


def _lane_blocks(y):
    return [y[:, j * LANES:(j + 1) * LANES] for j in range(y.shape[1] // LANES)]


def _residue_major(ref, dil):
    nblk, tm, _ = ref.shape
    per = tm // dil
    return jnp.concatenate(
        [jnp.concatenate([ref[j, pl.ds(r, per, stride=dil), :] for j in range(nblk)], axis=1)
         for r in range(dil)], axis=0)


def _store_token_order(buf, y, dil):
    per = y.shape[0] // dil
    for r in range(dil):
        for j, blk in enumerate(_lane_blocks(y[r * per:(r + 1) * per])):
            buf[j, pl.ds(r, per, stride=dil), :] = blk


def _proj_body(*refs, plan, first_tail_tile, dils):
    x_ref, g_ref, cos_ref, sin_ref, b_ref, wdec_ref, bdec_ref = refs[:7]
    w_refs = refs[7:7 + len(plan)]
    nout = sum(2 if p[4] else 1 for p in plan)
    out_refs = list(refs[7 + len(plan):7 + len(plan) + nout])
    scratch = refs[7 + len(plan) + nout:]
    tm, dm = x_ref.shape
    x = x_ref[...]
    xg = x * g_ref[...]
    inv_rms = jnp.broadcast_to(lax.rsqrt(jnp.mean(x * x, axis=-1, keepdims=True) + EPS),
                               (tm, LANES))
    if scratch:
        n_scr, tail_scr = scratch
        for j, blk in enumerate(_lane_blocks(xg) + [inv_rms]):
            n_scr[j] = blk
    rows_of = {1: xg.astype(BF16)}
    factor = {1: inv_rms}
    tables = {1: (cos_ref[...] * inv_rms, sin_ref[...] * inv_rms)}
    for d in dils:
        if d > 1:
            permuted = _residue_major(n_scr, d)
            rows_of[d] = permuted[:, :dm].astype(BF16)
            factor[d] = permuted[:, dm:]
            tables[d] = tuple(
                jnp.concatenate([t[pl.ds(r, tm // d, stride=d), :] for r in range(d)], axis=0)
                * factor[d] for t in (cos_ref, sin_ref))
    in_tail = pl.program_id(1) >= first_tail_tile
    oi = 0
    for (kind, scale, width, boff, has_tail, dil), w_ref in zip(plan, w_refs):
        out = out_refs[oi]
        oi += 1
        tail = None
        if has_tail:
            tail = out_refs[oi]
            oi += 1
        n = rows_of[dil]
        if kind == "decay":
            ga = (_mm(n, w_ref[...]) * factor[dil]).astype(BF16)
            z = _mm(ga, wdec_ref[...]) + bdec_ref[...]
            logsig = jnp.minimum(z, 0.0) - jnp.log1p(jnp.exp(-jnp.abs(z)))
            out[...] = logsig * (1.0 / GLA_GATE_NORM)
            continue
        per = tm // dil
        for c0 in range(0, width, COL_CHUNK):
            cw = min(COL_CHUNK, width - c0)
            y = _mm(n, w_ref[:, c0:c0 + cw])
            if kind == "rope":
                cosv, sinv = tables[dil]
                heads = []
                for j in range(cw // HEAD_DIM):
                    yh = y[:, j * HEAD_DIM:(j + 1) * HEAD_DIM]
                    heads.append(yh * cosv + pltpu.roll(yh, HEAD_DIM // 2, 1) * sinv)
                y = jnp.concatenate(heads, axis=1)
            else:
                y = y * jnp.concatenate([factor[dil]] * (cw // LANES), axis=1)
            if kind == "sig":
                y = jax.nn.sigmoid(y + b_ref[:, boff + c0:boff + c0 + cw])
            if tail is not None:
                @pl.when(in_tail)
                def _():
                    if dil == 1:
                        tail[:, c0:c0 + cw] = y
                    else:
                        _store_token_order(tail_scr, y, dil)
                        for j in range(cw // LANES):
                            tail[:, c0 + j * LANES:c0 + (j + 1) * LANES] = tail_scr[j]
            if scale != 1.0:
                y = y * scale
            for r in range(dil):
                out[:, r * width + c0:r * width + c0 + cw] = y[r * per:(r + 1) * per].astype(out.dtype)


def _project(x, gain, cos, sin, segs, *, tm, tail_rows=0, bias=None, wdec=None, bdec=None,
             rolls=()):
    B, L, D = x.shape
    nL = L // tm
    ntail = tail_rows // tm
    first_tail_tile = nL - ntail
    if bias is None:
        bias = jnp.zeros((1, LANES), F32)
    if wdec is None:
        wdec = jnp.zeros((LANES, LANES), BF16)
        bdec = jnp.zeros((1, LANES), F32)

    def const2(shape):
        return pl.BlockSpec(shape, lambda b, i: (0, 0))

    in_specs = [
        pl.BlockSpec((None, tm, D), lambda b, i: (b, i, 0)),
        const2((1, D)),
        pl.BlockSpec((tm, LANES), lambda b, i: (i, 0)),
        pl.BlockSpec((tm, LANES), lambda b, i: (i, 0)),
        const2(bias.shape), const2(wdec.shape), const2(bdec.shape),
    ]
    plan, weights, out_shapes, out_specs = [], [], [], []
    for kind, w, scale, dtype, boff, has_tail, dil in segs:
        width = w.shape[1]
        owidth = wdec.shape[1] if kind == "decay" else width
        plan.append((kind, scale, width, boff, has_tail, dil))
        weights.append(w)
        in_specs.append(const2(w.shape))
        out_shapes.append(jax.ShapeDtypeStruct((B, L // dil, dil * owidth), dtype))
        out_specs.append(pl.BlockSpec((None, tm // dil, dil * owidth), lambda b, i: (b, i, 0)))
        if has_tail:
            out_shapes.append(jax.ShapeDtypeStruct((B, tail_rows, width), F32))
            out_specs.append(pl.BlockSpec(
                (None, tm, width),
                lambda b, i: (b, jnp.maximum(i - first_tail_tile, 0), 0)))
    dils = tuple(sorted({p[5] for p in plan}))
    scratch = []
    if dils != (1,):
        scratch = [pltpu.VMEM((D // LANES + 1, tm, LANES), F32),
                   pltpu.VMEM((COL_CHUNK // LANES, tm, LANES), F32)]
    nsteps = B * nL
    extra, extra_specs, roll_shapes, roll_specs, roll_scratch = _roll_specs(rolls, nsteps)
    body = functools.partial(_proj_body, plan=tuple(plan), first_tail_tile=first_tail_tile,
                             dils=dils)
    body = _with_rolls(body, len(in_specs), len(out_shapes), len(rolls), nsteps,
                       lambda: pl.program_id(0) * nL + pl.program_id(1))
    out = pl.pallas_call(
        body, out_shape=out_shapes + roll_shapes, grid=(B, nL),
        in_specs=in_specs + extra_specs, out_specs=out_specs + roll_specs,
        scratch_shapes=roll_scratch + scratch,
        compiler_params=_params(("arbitrary", "arbitrary")), name="project",
    )(x, gain, cos, sin, bias, wdec, bdec, *weights, *extra)
    return out[:len(out_shapes)], out[len(out_shapes):]


GLA_SUB = 8


def _gla_level_masks(T):
    row = lax.broadcasted_iota(jnp.int32, (T, T), 0)
    col = lax.broadcasted_iota(jnp.int32, (T, T), 1)
    rid = lax.broadcasted_iota(jnp.int32, (T, 1), 0)
    levels = []
    bs = 2 * GLA_SUB
    while bs <= T:
        half = bs // 2
        second = (rid & (bs - 1)) >= half
        pair = ((row & -bs) == (col & -bs)) & ((row & (bs - 1)) >= half) & ((col & (bs - 1)) < half)
        levels.append((second, pair, bs))
        bs *= 2
    return levels


def _gla_matmuls(qf, kf, v, cum, s0, levels):
    T = qf.shape[0]
    last = cum[T - 1:T, :]
    o_inter = _mm((qf * jnp.exp2(cum)).astype(BF16), s0.astype(BF16))
    off = jnp.zeros((T, T), F32)
    for second, pair_mask, bs in levels:
        bound = []
        for j in range(T // bs):
            mid_row = j * bs + bs // 2 - 1
            mid = jnp.broadcast_to(cum[mid_row:mid_row + 1, :], (GLA_SUB, HEAD_DIM))
            bound += [mid] * (bs // GLA_SUB)
        rel = cum - jnp.concatenate(bound, axis=0)
        e = jnp.exp2(jnp.where(second, rel, -rel))
        off = jnp.where(pair_mask, _mm_nt((qf * e).astype(BF16), (kf * e).astype(BF16)), off)
    upd = _mm_tn((kf * jnp.exp2(last - cum)).astype(BF16), v)
    er = lax.broadcasted_iota(jnp.int32, (HEAD_DIM, HEAD_DIM), 0)
    ec = lax.broadcasted_iota(jnp.int32, (HEAD_DIM, HEAD_DIM), 1)
    decay_col = jnp.sum(jnp.where(er == ec, jnp.exp2(last), 0.0), axis=1, keepdims=True)
    return o_inter, off, decay_col * s0 + upd


def _gla_diagonal(qf, kf, cum, off, diag_masks):
    T = qf.shape[0]
    tiles = (T // GLA_SUB, GLA_SUB, HEAD_DIM)
    q3, k3, c3 = qf.reshape(tiles), kf.reshape(tiles), cum.reshape(tiles)
    att3 = off.reshape(T // GLA_SUB, GLA_SUB, T)
    for j in range(GLA_SUB):
        if j == 0:
            prod = q3 * k3
        else:
            prod = q3 * pltpu.roll(k3, j, 1) * jnp.exp2(c3 - pltpu.roll(c3, j, 1))
        att3 = jnp.where(diag_masks[j], jnp.sum(prod, axis=-1, keepdims=True), att3)
    return att3.reshape(T, T)


def _gla_body(q_ref, k_ref, v_ref, lf_ref, r_ref, s0_ref, hn_ref, a_ref, s_ref, *, nb):
    T = q_ref.shape[1]
    hw = HEADS * HEAD_DIM
    first = pl.program_id(1) == 0
    row = lax.broadcasted_iota(jnp.int32, (T, T), 0)
    col = lax.broadcasted_iota(jnp.int32, (T, T), 1)
    tri = jnp.where(row >= col, 1.0, 0.0).astype(BF16)
    levels = _gla_level_masks(T)
    shape3 = (T // GLA_SUB, GLA_SUB, T)
    sub_row = lax.broadcasted_iota(jnp.int32, shape3, 1)
    token = lax.broadcasted_iota(jnp.int32, shape3, 0) * GLA_SUB + sub_row
    key = lax.broadcasted_iota(jnp.int32, shape3, 2)
    diag_masks = [(key == token - j) & (sub_row >= j) for j in range(GLA_SUB)]

    @pl.when(first)
    def _():
        s_ref[...] = s0_ref[...]

    cums = []
    for b in range(nb):
        lf = lf_ref[b] * LOG2_E
        hi = lf.astype(BF16)
        lo = (lf - hi.astype(F32)).astype(BF16)
        parts = _mm(tri, jnp.concatenate([hi, lo], axis=1))
        cums.append(parts[:, :hw] + parts[:, hw:])
    pairs = [(b, h) for b in range(nb) for h in range(HEADS)]
    dk = lambda h: slice(h * HEAD_DIM, (h + 1) * HEAD_DIM)
    dv = lambda h: slice(h * GLA_DV, (h + 1) * GLA_DV)
    qk = {(b, h): (q_ref[b, :, dk(h)].astype(F32), k_ref[b, :, dk(h)].astype(F32), cums[b][:, dk(h)])
          for b, h in pairs}
    stage1 = {}
    for b, h in pairs:
        qf, kf, cum = qk[b, h]
        stage1[b, h] = _gla_matmuls(qf, kf, v_ref[b, :, dv(h)], cum, s_ref[b, h], levels)
        s_ref[b, h] = stage1[b, h][2]
    att = {}
    for b, h in pairs:
        qf, kf, cum = qk[b, h]
        att[b, h] = _gla_diagonal(qf, kf, cum, stage1[b, h][1], diag_masks).astype(BF16)
    for b, h in pairs:
        o = stage1[b, h][0] + _mm(att[b, h], v_ref[b, :, dv(h)])
        r = r_ref[b, :, dv(h)].astype(F32)
        a_ref[b, :, dv(h)] = (_rms(o, hn_ref[...]) * (r * jax.nn.sigmoid(r))).astype(a_ref.dtype)


def _gla(q, k, v, lf, r, s0, head_norm, *, chunk, nb):
    B, L, hw = q.shape
    tok = lambda w: pl.BlockSpec((nb, chunk, w), lambda b, c: (b, c, 0))
    state = pl.BlockSpec((nb, HEADS, HEAD_DIM, GLA_DV), lambda b, c: (b, 0, 0, 0))
    return pl.pallas_call(
        functools.partial(_gla_body, nb=nb),
        out_shape=[jax.ShapeDtypeStruct((B, L, HEADS * GLA_DV), BF16),
                   jax.ShapeDtypeStruct(s0.shape, F32)],
        grid=(B // nb, L // chunk),
        in_specs=[tok(512), tok(512), tok(1024), tok(512), tok(1024), state,
                  pl.BlockSpec((1, GLA_DV), lambda b, c: (0, 0))],
        out_specs=[tok(1024), state],
        compiler_params=_params(("parallel", "arbitrary")), name="gla",
    )(q, k, v, lf, r, s0, head_norm)


def _band_body(q_ref, k_ref, v_ref, kp_ref, vp_ref, o_ref, lse_ref, *, tq):
    step = pl.program_id(2)
    nblk = tq // WINDOW_KEYS
    t = lax.broadcasted_iota(jnp.int32, (WINDOW_KEYS, 2 * WINDOW_KEYS), 0)
    c = lax.broadcasted_iota(jnp.int32, (WINDOW_KEYS, 2 * WINDOW_KEYS), 1)
    band = (c >= t) & (c <= t + WINDOW_KEYS)
    lane = lax.broadcasted_iota(jnp.int32, (WINDOW_KEYS, LANES), 1)
    rows = lambda j: slice(j * WINDOW_KEYS, (j + 1) * WINDOW_KEYS)
    hd = lambda h: slice(h * HEAD_DIM, (h + 1) * HEAD_DIM)

    def window(cur_ref, prev_ref, j, h):
        before = prev_ref[:, hd(h)] if j == 0 else cur_ref[rows(j - 1), hd(h)]
        return jnp.concatenate([before, cur_ref[rows(j), hd(h)]], axis=0)

    pairs = [(j, h) for j in range(nblk) for h in range(HEADS)]
    scores = [_mm_nt(q_ref[rows(j), hd(h)], window(k_ref, kp_ref, j, h)) for j, h in pairs]
    probs = []
    for (j, h), s in zip(pairs, scores):
        first_key = step * tq + (j - 1) * WINDOW_KEYS
        s = jnp.where(band & (c + first_key >= 0), s, NEG)
        m = jnp.max(s, axis=-1, keepdims=True)
        p = jnp.exp(s - m)
        den = jnp.sum(p, axis=-1, keepdims=True)
        probs.append((p.astype(BF16), den, m + jnp.log(den)))
    lse_blk = [jnp.zeros((WINDOW_KEYS, LANES), F32) for _ in range(nblk)]
    for (j, h), (p, den, lse) in zip(pairs, probs):
        o_ref[rows(j), hd(h)] = (_mm(p, window(v_ref, vp_ref, j, h)) / den).astype(o_ref.dtype)
        lse_blk[j] = jnp.where(lane == h, lse, lse_blk[j])
    for j in range(nblk):
        lse_ref[rows(j), :] = lse_blk[j][:, :HEADS]


def _band_attention(q, k, v, dil, *, tq):
    B, n, _ = q.shape
    W = HEADS * HEAD_DIM
    per = tq // WINDOW_KEYS
    cur = pl.BlockSpec((None, tq, W), lambda b, r, i: (b, i, r))
    prv = pl.BlockSpec((None, WINDOW_KEYS, W), lambda b, r, i: (b, jnp.maximum(i * per - 1, 0), r))
    o, lse = pl.pallas_call(
        functools.partial(_band_body, tq=tq),
        out_shape=[jax.ShapeDtypeStruct((B, n, dil * W), BF16),
                   jax.ShapeDtypeStruct((B, dil, n, HEADS), F32)],
        grid=(B, dil, n // tq),
        in_specs=[cur, cur, cur, prv, prv],
        out_specs=[cur, pl.BlockSpec((None, None, tq, HEADS), lambda b, r, i: (b, r, i, 0))],
        compiler_params=_params(("parallel", "parallel", "arbitrary")), name="band_attention",
    )(q, k, v, k, v)
    return o, lse.transpose(0, 2, 1, 3).reshape(B, n * dil, HEADS)


def _cross_body(q_ref, mk_ref, mv_ref, o_ref, *, nb):
    slots = mk_ref.shape[1] // HEADS
    pairs = [(b, h) for b in range(nb) for h in range(HEADS)]
    head_rows = lambda h: pl.ds(h, slots, stride=HEADS)
    lanes = lambda h: slice(h * HEAD_DIM, (h + 1) * HEAD_DIM)
    scores = [_mm_nt(q_ref[b, :, lanes(h)], mk_ref[b, head_rows(h), :].astype(BF16))
              for b, h in pairs]
    probs = []
    for s in scores:
        p = jnp.exp(s - jnp.max(s, axis=-1, keepdims=True))
        probs.append((p.astype(BF16), jnp.sum(p, axis=-1, keepdims=True)))
    for (b, h), (p, den) in zip(pairs, probs):
        pv = _mm(p, mv_ref[b, head_rows(h), :].astype(BF16))
        o_ref[b, :, lanes(h)] = (pv / den).astype(o_ref.dtype)


def _cross_attention(q, mk, mv, *, tq, nb):
    B, L, W = q.shape
    tok = pl.BlockSpec((nb, tq, W), lambda b, i: (b, i, 0))
    mem = pl.BlockSpec((nb,) + mk.shape[1:], lambda b, i: (b, 0, 0))
    return pl.pallas_call(
        functools.partial(_cross_body, nb=nb), out_shape=jax.ShapeDtypeStruct((B, L, W), BF16),
        grid=(B // nb, L // tq), in_specs=[tok, mem, mem], out_specs=tok,
        compiler_params=_params(("parallel", "parallel")), name="cross_attention",
    )(q, mk, mv)


DECODE_ROW_GROUP = 16


DECODE_ROWS = 16


def _decode_body(q_ref, kn_ref, vn_ref, ck_ref, cv_ref, o_ref, lse_ref, *, dil, nnew, nb):
    grouped = len(ck_ref.shape) == 4
    nk = (nnew if grouped else dil) * WINDOW_KEYS
    qrow = lax.broadcasted_iota(jnp.int32, (DECODE_ROWS, nk), 0)
    kcol = lax.broadcasted_iota(jnp.int32, (DECODE_ROWS, nk), 1)
    if dil == 1:
        visible = kcol >= qrow
    elif grouped:
        visible = (kcol // WINDOW_KEYS) == qrow
    else:
        visible = (kcol % dil) == qrow
    new_q = lax.broadcasted_iota(jnp.int32, (nnew, 1), 0)
    head_new = lambda h: pl.ds(h, nnew, stride=HEADS)

    def cached(ref, b, h):
        if grouped:
            return jnp.concatenate([ref[b, :, c * HEADS + h, :] for c in range(nnew)], axis=0)
        return ref[b, pl.ds(h, nk, stride=HEADS), :]

    pairs = [(b, h) for b in range(nb) for h in range(HEADS)]
    pad = jnp.zeros((DECODE_ROWS - nnew, LANES), F32)
    queries = {bh: q_ref[bh[0], head_new(bh[1]), :] for bh in pairs}
    scores = {bh: _mm_nt(jnp.concatenate([queries[bh], pad], axis=0).astype(BF16),
                         cached(ck_ref, *bh).astype(BF16)) for bh in pairs}
    soft = {}
    for b, h in pairs:
        s = jnp.where(visible, scores[b, h], NEG)[:nnew]
        kn = kn_ref[b, head_new(h), :]
        sn = []
        for c in range(nnew):
            ok = (new_q >= c) if dil == 1 else (new_q == c)
            sn.append(jnp.where(ok, jnp.sum(queries[b, h] * kn[c:c + 1, :], axis=-1, keepdims=True),
                                NEG))
        m = jnp.max(s, axis=-1, keepdims=True)
        for t in sn:
            m = jnp.maximum(m, t)
        p = jnp.exp(s - m)
        pn = [jnp.exp(t - m) for t in sn]
        den = jnp.sum(p, axis=-1, keepdims=True)
        for t in pn:
            den = den + t
        p_rows = jnp.concatenate([p, jnp.zeros((DECODE_ROWS - nnew, nk), F32)], axis=0)
        soft[b, h] = (p_rows.astype(BF16), pn, den, m + jnp.log(den))
    for b, h in pairs:
        p, pn, den, lse = soft[b, h]
        acc = _mm(p, cached(cv_ref, b, h).astype(BF16))[:nnew]
        vn = vn_ref[b, head_new(h), :]
        for c in range(nnew):
            acc = acc + pn[c] * vn[c:c + 1, :]
        o_ref[b, head_new(h), :] = acc / den
        lse_ref[b, head_new(h), :] = jnp.broadcast_to(lse, (nnew, LANES))


def _decode_attention(q, kn, vn, ck, cv, window, dil, *, nb):
    B, nrow, _ = q.shape
    rows = window * HEADS
    small = pl.BlockSpec((nb, nrow, LANES), lambda b: (b, 0, 0))
    slot_rows = HEADS * dil
    if slot_rows > DECODE_ROW_GROUP:
        view = lambda c: c.reshape(B, WINDOW_KEYS, slot_rows, LANES)
        big = pl.BlockSpec((nb, WINDOW_KEYS, DECODE_ROW_GROUP, LANES), lambda b: (b, 0, 0, 0))
    else:
        view = lambda c: c
        big = pl.BlockSpec((nb, rows, LANES), lambda b: (b, 0, 0))
    return pl.pallas_call(
        functools.partial(_decode_body, dil=dil, nnew=nrow // HEADS, nb=nb),
        out_shape=[jax.ShapeDtypeStruct((B, nrow, LANES), F32),
                   jax.ShapeDtypeStruct((B, nrow, LANES), F32)],
        grid=(B // nb,), in_specs=[small, small, small, big, big],
        out_specs=[small, small],
        compiler_params=_params(("parallel",)), name="decode_attention",
    )(q, kn, vn, view(ck), view(cv))


def _roll_copies(step, nsteps, olds, news, outs, sems):
    copies = []
    for t, (old, new, out) in enumerate(zip(olds, news, outs)):
        total, rows, _ = old.shape
        shift = new.shape[1]
        per = total // nsteps
        batches = pl.ds(step * per, per)
        copies.append(pltpu.make_async_copy(
            old.at[batches, pl.ds(shift, rows - shift), :],
            out.at[batches, pl.ds(0, rows - shift), :], sems.at[2 * t]))
        copies.append(pltpu.make_async_copy(
            new.at[batches], out.at[batches, pl.ds(rows - shift, shift), :], sems.at[2 * t + 1]))
    return copies


def _with_rolls(body, nfixed_in, nfixed_out, nrolls, nsteps, step_fn):
    if nrolls == 0:
        return body

    def wrapped(*refs):
        ins = refs[:nfixed_in]
        pairs = refs[nfixed_in:nfixed_in + 2 * nrolls]
        outs = refs[nfixed_in + 2 * nrolls:nfixed_in + 2 * nrolls + nfixed_out]
        rolled = refs[nfixed_in + 2 * nrolls + nfixed_out:nfixed_in + 3 * nrolls + nfixed_out]
        sems = refs[nfixed_in + 3 * nrolls + nfixed_out]
        rest = refs[nfixed_in + 3 * nrolls + nfixed_out + 1:]
        copies = _roll_copies(step_fn(), nsteps, pairs[0::2], pairs[1::2], rolled, sems)
        for c in copies:
            c.start()
        body(*ins, *outs, *rest)
        for c in copies:
            c.wait()

    return wrapped


def _roll_specs(rolls, nsteps):
    any_spec = pl.BlockSpec(memory_space=pl.ANY)
    extra, shapes = [], []
    for old, new in rolls:
        assert old.shape[0] % nsteps == 0
        extra += [old, new]
        shapes.append(jax.ShapeDtypeStruct(old.shape, old.dtype))
    scratch = [pltpu.SemaphoreType.DMA((2 * len(rolls),))] if rolls else []
    return extra, [any_spec] * len(extra), shapes, [any_spec] * len(shapes), scratch


def _merge_body(x_ref, a_ref, o1_ref, o2_ref, o3_ref, l1_ref, l2_ref, l3_ref, ox_ref, g_ref,
                wa_ref, wb_ref, wc_ref, wo_ref, h_ref, *scratch, dils):
    tm = x_ref.shape[0]
    hw = HEADS * HEAD_DIM
    groups, si = [], 0
    for o_ref, d in zip((o1_ref, o2_ref, o3_ref), dils):
        if d == 1:
            groups.append([o_ref[:, h * HEAD_DIM:(h + 1) * HEAD_DIM] for h in range(HEADS)])
            continue
        buf = scratch[si]
        si += 1
        for r in range(d):
            for h in range(HEADS):
                lanes = slice(r * hw + h * HEAD_DIM, r * hw + (h + 1) * HEAD_DIM)
                buf[h, pl.ds(r, tm // d, stride=d), :] = o_ref[:, lanes].astype(F32)
        groups.append([buf[h] for h in range(HEADS)])
    l1, l2, l3 = l1_ref[...], l2_ref[...], l3_ref[...]
    lmax = jnp.maximum(jnp.maximum(l1, l2), l3)
    e1, e2, e3 = jnp.exp(l1 - lmax), jnp.exp(l2 - lmax), jnp.exp(l3 - lmax)
    inv = 1.0 / (e1 + e2 + e3)
    heads = []
    for h in range(HEADS):
        heads.append((e1[:, h:h + 1] * inv[:, h:h + 1]) * groups[0][h].astype(F32)
                     + (e2[:, h:h + 1] * inv[:, h:h + 1]) * groups[1][h].astype(F32)
                     + (e3[:, h:h + 1] * inv[:, h:h + 1]) * groups[2][h].astype(F32))
    o_dil = jnp.concatenate(heads, axis=1).astype(BF16)
    ya = _mm(a_ref[...], wa_ref[...])
    yb = _mm(o_dil, wb_ref[...])
    yc = _mm(ox_ref[...], wc_ref[...])
    d = D_MODEL
    mix = (g_ref[:, 0:d].astype(F32) * ya + g_ref[:, d:2 * d].astype(F32) * yb
           + g_ref[:, 2 * d:3 * d].astype(F32) * yc)
    h_ref[...] = x_ref[...] + _mm(mix.astype(BF16), wo_ref[...])


def _merge(x, a, o_groups, lse_groups, ox, gates, wa, wb, wc, wo, *, tm, dils, rolls=()):
    T = x.shape[0]
    nsteps = T // tm
    hw = HEADS * HEAD_DIM
    tok = lambda w: pl.BlockSpec((tm, w), lambda i: (i, 0))
    grp = lambda d: pl.BlockSpec((tm // d, d * hw), lambda i: (i, 0))
    full = lambda w: pl.BlockSpec(w.shape, lambda i: (0, 0))
    extra, extra_specs, roll_shapes, roll_specs, roll_scratch = _roll_specs(rolls, nsteps)
    body = _with_rolls(functools.partial(_merge_body, dils=dils), 14, 1, len(rolls), nsteps,
                       lambda: pl.program_id(0))
    out = pl.pallas_call(
        body, out_shape=[jax.ShapeDtypeStruct((T, D_MODEL), F32)] + roll_shapes, grid=(nsteps,),
        in_specs=[tok(D_MODEL), tok(HEADS * GLA_DV), grp(dils[0]), grp(dils[1]), grp(dils[2]),
                  tok(HEADS), tok(HEADS), tok(HEADS), tok(hw), tok(3 * D_MODEL),
                  full(wa), full(wb), full(wc), full(wo)] + extra_specs,
        out_specs=[tok(D_MODEL)] + roll_specs,
        scratch_shapes=roll_scratch + [pltpu.VMEM((HEADS, tm, HEAD_DIM), F32)
                                       for d in dils if d > 1],
        compiler_params=_params(("arbitrary",)), name="merge",
    )(x, a, *o_groups, *lse_groups, ox, gates, wa, wb, wc, wo, *extra)
    return out[0], out[1:]


def _ffn_body(h_ref, gn_ref, wu_ref, wd_ref, gf_ref, y_ref):
    h = h_ref[...]
    n = (h * gn_ref[...]).astype(BF16)
    inv_ms = 1.0 / (jnp.mean(h * h, axis=-1, keepdims=True) + EPS)
    acc = None
    for c0 in range(0, D_FF, COL_CHUNK):
        u = jnp.maximum(_mm(n, wu_ref[:, c0:c0 + COL_CHUNK]), 0.0)
        part = _mm((u * u).astype(BF16), wd_ref[c0:c0 + COL_CHUNK, :])
        acc = part if acc is None else acc + part
    y_ref[...] = _rms(h + inv_ms * acc, gf_ref[...])


def _ffn(h, gain_ffn, wu, wd, gain_final, *, tm, rolls=()):
    T = h.shape[0]
    nsteps = T // tm
    tok = pl.BlockSpec((tm, D_MODEL), lambda i: (i, 0))
    full = lambda w: pl.BlockSpec(w.shape, lambda i: (0, 0))
    extra, extra_specs, roll_shapes, roll_specs, roll_scratch = _roll_specs(rolls, nsteps)
    body = _with_rolls(_ffn_body, 5, 1, len(rolls), nsteps, lambda: pl.program_id(0))
    out = pl.pallas_call(
        body, out_shape=[jax.ShapeDtypeStruct((T, D_MODEL), F32)] + roll_shapes, grid=(nsteps,),
        in_specs=[tok, full(gain_ffn), full(wu), full(wd), full(gain_final)] + extra_specs,
        out_specs=[tok] + roll_specs, scratch_shapes=roll_scratch,
        compiler_params=_params(("arbitrary",)), name="ffn",
    )(h, gain_ffn, wu, wd, gain_final, *extra)
    return out[0], out[1:]


def _rope_tables(pos):
    half = HEAD_DIM // 2
    inv = ROPE_THETA ** (-jnp.arange(half, dtype=F32) / half)
    ang = pos.astype(F32)[:, None] * inv[None, :]
    cos, sin = jnp.cos(ang), jnp.sin(ang)
    return jnp.concatenate([cos, cos], axis=-1), jnp.concatenate([-sin, sin], axis=-1)


def _split_weights(w_in, w_decay):
    hw = HEADS * HEAD_DIM
    sizes = [hw, hw, HEADS * GLA_DV, HEADS * GLA_DV, GLA_RANK] + [hw] * 9 + [hw, 3 * D_MODEL]
    offs = [0]
    for s in sizes:
        offs.append(offs[-1] + s)
    cols = [w_in[:, offs[j]:offs[j + 1]].astype(BF16) for j in range(len(sizes))]
    cols[4] = jnp.pad(cols[4], ((0, 0), (0, LANES - GLA_RANK)))
    wdec = jnp.pad(w_decay.astype(BF16), ((0, LANES - GLA_RANK), (0, 0)))
    return cols, wdec


def _layer(x, pos, tail_rows, cols, wdec, b_decay, b_gate, norm_mix, tm, dils, rolls=()):
    cos, sin = _rope_tables(pos)
    gain = norm_mix[None, :]
    (gq, gk, gv, gr, lf), _ = _project(
        x, gain, cos, sin,
        [("lin", cols[0], QK_SCALE, BF16, 0, False, 1), ("lin", cols[1], 1.0, BF16, 0, False, 1),
         ("lin", cols[2], 1.0, BF16, 0, False, 1), ("lin", cols[3], 1.0, BF16, 0, False, 1),
         ("decay", cols[4], 1.0, F32, 0, False, 1)],
        tm=tm, wdec=wdec, bdec=b_decay[None, :])
    dil_segs = []
    for g in range(3):
        dil_segs += [("rope", cols[5 + 3 * g], QK_SCALE, BF16, 0, False, dils[g]),
                     ("rope", cols[6 + 3 * g], 1.0, BF16, 0, True, dils[g]),
                     ("lin", cols[7 + 3 * g], 1.0, BF16, 0, True, dils[g])]
    dil_out, rolled = _project(x, gain, cos, sin, dil_segs, tm=tm, tail_rows=tail_rows,
                               rolls=rolls)
    (xq, gates), _ = _project(
        x, gain, cos, sin,
        [("lin", cols[14], QK_SCALE, BF16, 0, False, 1), ("sig", cols[15], 1.0, BF16, 0, False, 1)],
        tm=tm, bias=b_gate[None, :])
    dil = [dil_out[5 * g:5 * g + 5] for g in range(3)]
    return (gq, gk, gv, gr, lf), dil, xq, gates, rolled


def _finish(x2d, a, o_groups, lse_groups, ox, gates, post, tm, dils, merge_rolls=(),
            ffn_rolls=()):
    wa, wb, wc, wo, gain_ffn, wu, wd, gain_final = post
    flat = lambda t: t.reshape(-1, t.shape[-1])
    h, rolled_m = _merge(x2d, flat(a), [flat(t) for t in o_groups], [flat(t) for t in lse_groups],
                         flat(ox), flat(gates), wa, wb, wc, wo, tm=tm, dils=dils,
                         rolls=merge_rolls)
    y, rolled_f = _ffn(h, gain_ffn, wu, wd, gain_final, tm=tm, rolls=ffn_rolls)
    return y, rolled_m, rolled_f


def kernel(x_prompt, x_sample, mem_prompt, state_gla, cache_dil1_k, cache_dil1_v, cache_dil2_k, cache_dil2_v, cache_dil3_k, cache_dil3_v, cache_mem_k, cache_mem_v, norm_mix, w_in, b_gate, w_decay, b_decay, gla_head_norm, w_proj_gla, w_proj_dil, w_proj_x, norm_mem, w_mem_kv, w_out, norm_ffn, w_ffn_up, w_ffn_down, norm_final):
    B, L, D = x_prompt.shape
    SB, SL, _ = x_sample.shape
    depth = w_in.shape[0]
    assert depth == 1, "single trunk layer"
    i = 0
    cols, wdec = _split_weights(w_in[i], w_decay[i])
    post = (w_proj_gla[i].astype(BF16), w_proj_dil[i].astype(BF16), w_proj_x[i].astype(BF16),
            w_out[i].astype(BF16), norm_ffn[i][None, :], w_ffn_up[i].astype(BF16),
            w_ffn_down[i].astype(BF16), norm_final[None, :])
    head_norm = gla_head_norm[i][None, :]
    hw = HEADS * HEAD_DIM
    max_window = DIL_PATTERNS[-1][0]

    mem_rows = lambda t: t.reshape(t.shape[0], N_MEM * HEADS, HEAD_DIM)

    T = SB * SL
    pos_s = jnp.tile(PAST_LEN + jnp.arange(SL), SB)
    (gq, gk, gv, gr, lf), dil, xq, gates, _ = _layer(
        x_sample.reshape(1, T, D), pos_s, T, cols, wdec, b_decay[i], b_gate[i], norm_mix[i], T,
        (1, 1, 1))
    pad_rows = 16 - SL
    per_batch = lambda t: jnp.pad(t.reshape(SB, SL, t.shape[-1]), ((0, 0), (0, pad_rows), (0, 0)))
    a_s, state_s = _gla(per_batch(gq), per_batch(gk), per_batch(gv), per_batch(lf), per_batch(gr),
                        state_gla[i], head_norm, chunk=16, nb=8)
    a_s = a_s[:, :SL]
    caches = ((cache_dil1_k[i], cache_dil1_v[i]), (cache_dil2_k[i], cache_dil2_v[i]),
              (cache_dil3_k[i], cache_dil3_v[i]))
    o_groups, lse_groups, roll_pairs = [], [], []
    rows = lambda t: t.astype(F32).reshape(SB, SL * HEADS, HEAD_DIM)
    for (win, dl), (dq, dk, dk_tail, dv, dv_tail), (ck, cv), nb in zip(
            DIL_PATTERNS, dil, caches, (4, 2, 2)):
        ck, cv = (c.reshape(SB, win * HEADS, HEAD_DIM) for c in (ck, cv))
        o_g, lse_g = _decode_attention(rows(dq), rows(dk_tail), rows(dv_tail), ck, cv, win, dl,
                                       nb=nb)
        o_groups.append(o_g.reshape(SB, SL, hw).astype(BF16))
        lse_groups.append(lse_g[:, :, 0].reshape(SB, SL, HEADS))
        roll_pairs += [(ck, rows(dk_tail)), (cv, rows(dv_tail))]
    ox = _cross_attention(per_batch(xq), mem_rows(cache_mem_k[i]), mem_rows(cache_mem_v[i]),
                          tq=16, nb=8)[:, :SL]
    y_sample, _, _ = _finish(x_sample.reshape(T, D), a_s, o_groups, lse_groups, ox, gates, post,
                             T, (1, 1, 1))
    y_sample = y_sample.reshape(SB, SL, D)

    dils_p = tuple(dl for _, dl in DIL_PATTERNS)
    (gq, gk, gv, gr, lf), dil, xq, gates, rolled_v3 = _layer(
        x_prompt, jnp.arange(L), max_window, cols, wdec, b_decay[i], b_gate[i], norm_mix[i], 512,
        dils_p, rolls=roll_pairs[5:6])
    a_p, state_p = _gla(gq, gk, gv, lf, gr, jnp.zeros((B, HEADS, HEAD_DIM, GLA_DV), F32),
                        head_norm, chunk=128, nb=1)
    o_groups, lse_groups, bufs_p = [], [], []
    for (win, dl), (dq, dk, dk_tail, dv, dv_tail) in zip(DIL_PATTERNS, dil):
        o_g, lse_g = _band_attention(dq, dk, dv, dl, tq=512)
        o_groups.append(o_g)
        lse_groups.append(lse_g)
        keep = min(win, L)
        bufs_p += [dk_tail[:, max_window - keep:].reshape(1, B, keep, HEADS, HEAD_DIM),
                   dv_tail[:, max_window - keep:].reshape(1, B, keep, HEADS, HEAD_DIM)]
    zeros_tab = jnp.zeros((N_MEM, LANES), F32)
    w_mem = w_mem_kv[i].astype(BF16)
    (mk, mv), _ = _project(
        mem_prompt, norm_mem[i][None, :], zeros_tab, zeros_tab,
        [("lin", w_mem[:, :hw], 1.0, F32, 0, False, 1), ("lin", w_mem[:, hw:], 1.0, F32, 0, False, 1)],
        tm=N_MEM)
    ox = _cross_attention(xq, mem_rows(mk), mem_rows(mv), tq=512, nb=1)
    y_prompt, rolled_12, rolled_k3 = _finish(
        x_prompt.reshape(B * L, D), a_p, o_groups, lse_groups, ox, gates, post, 512, dils_p,
        merge_rolls=roll_pairs[:4], ffn_rolls=roll_pairs[4:5])
    y_prompt = y_prompt.reshape(B, L, D)
    bufs_s = [c.reshape(1, SB, c.shape[1] // HEADS, HEADS, HEAD_DIM)
              for c in (*rolled_12, *rolled_k3, *rolled_v3)]

    return (y_prompt, y_sample, state_p[None], *bufs_p,
            mk.reshape(1, B, N_MEM, HEADS, HEAD_DIM), mv.reshape(1, B, N_MEM, HEADS, HEAD_DIM),
            state_s[None], *bufs_s)
```

```python
import functools

import jax
import jax.numpy as jnp
from jax import lax
from jax.experimental import pallas as pl
from jax.experimental.pallas import tpu as pltpu

F32 = jnp.float32
BF16 = jnp.bfloat16

D_MODEL = 1024
PAST_LEN = 8192
N_MEM = 256
HEADS = 4
HEAD_DIM = 128
GLA_DV = 256
GLA_RANK = 16
GLA_GATE_NORM = 16.0
DIL_PATTERNS = ((128, 1), (512, 4), (2048, 16))
WINDOW_KEYS = 128
D_FF = 4 * D_MODEL
ROPE_THETA = 10000.0
EPS = 1e-6
NEG = -1e30
QK_SCALE = HEAD_DIM ** -0.5
LOG2_E = 1.4426950408889634

LANES = 128
COL_CHUNK = 512
VMEM_LIMIT = 48 * 1024 * 1024


def _mm(a, b):
    return jnp.dot(a, b, preferred_element_type=F32)


def _mm_nt(a, b):
    return lax.dot_general(a, b, (((1,), (1,)), ((), ())), preferred_element_type=F32)


def _mm_tn(a, b):
    return lax.dot_general(a, b, (((0,), (0,)), ((), ())), preferred_element_type=F32)


def _rms(x, gain):
    return x * lax.rsqrt(jnp.mean(x * x, axis=-1, keepdims=True) + EPS) * gain


def _params(sem):
    return pltpu.CompilerParams(dimension_semantics=sem, vmem_limit_bytes=VMEM_LIMIT)


def _lane_blocks(y):
    return [y[:, j * LANES:(j + 1) * LANES] for j in range(y.shape[1] // LANES)]


def _residue_major(ref, dil):
    nblk, tm, _ = ref.shape
    per = tm // dil
    return jnp.concatenate(
        [jnp.concatenate([ref[j, pl.ds(r, per, stride=dil), :] for j in range(nblk)], axis=1)
         for r in range(dil)], axis=0)


def _store_token_order(buf, y, dil):
    per = y.shape[0] // dil
    for r in range(dil):
        for j, blk in enumerate(_lane_blocks(y[r * per:(r + 1) * per])):
            buf[j, pl.ds(r, per, stride=dil), :] = blk


def _proj_body(*refs, plan, first_tail_tile, dils):
    x_ref, g_ref, cos_ref, sin_ref, b_ref, wdec_ref, bdec_ref = refs[:7]
    w_refs = refs[7:7 + len(plan)]
    nout = sum(2 if p[4] else 1 for p in plan)
    out_refs = list(refs[7 + len(plan):7 + len(plan) + nout])
    scratch = refs[7 + len(plan) + nout:]
    tm, dm = x_ref.shape
    x = x_ref[...]
    xg = x * g_ref[...]
    inv_rms = jnp.broadcast_to(lax.rsqrt(jnp.mean(x * x, axis=-1, keepdims=True) + EPS),
                               (tm, LANES))
    if scratch:
        n_scr, tail_scr = scratch
        for j, blk in enumerate(_lane_blocks(xg) + [inv_rms]):
            n_scr[j] = blk
    rows_of = {1: xg.astype(BF16)}
    factor = {1: inv_rms}
    tables = {1: (cos_ref[...] * inv_rms, sin_ref[...] * inv_rms)}
    for d in dils:
        if d > 1:
            permuted = _residue_major(n_scr, d)
            rows_of[d] = permuted[:, :dm].astype(BF16)
            factor[d] = permuted[:, dm:]
            tables[d] = tuple(
                jnp.concatenate([t[pl.ds(r, tm // d, stride=d), :] for r in range(d)], axis=0)
                * factor[d] for t in (cos_ref, sin_ref))
    in_tail = pl.program_id(1) >= first_tail_tile
    oi = 0
    for (kind, scale, width, boff, has_tail, dil), w_ref in zip(plan, w_refs):
        out = out_refs[oi]
        oi += 1
        tail = None
        if has_tail:
            tail = out_refs[oi]
            oi += 1
        n = rows_of[dil]
        if kind == "decay":
            ga = (_mm(n, w_ref[...]) * factor[dil]).astype(BF16)
            z = _mm(ga, wdec_ref[...]) + bdec_ref[...]
            logsig = jnp.minimum(z, 0.0) - jnp.log1p(jnp.exp(-jnp.abs(z)))
            out[...] = logsig * (1.0 / GLA_GATE_NORM)
            continue
        per = tm // dil
        for c0 in range(0, width, COL_CHUNK):
            cw = min(COL_CHUNK, width - c0)
            y = _mm(n, w_ref[:, c0:c0 + cw])
            if kind == "rope":
                cosv, sinv = tables[dil]
                heads = []
                for j in range(cw // HEAD_DIM):
                    yh = y[:, j * HEAD_DIM:(j + 1) * HEAD_DIM]
                    heads.append(yh * cosv + pltpu.roll(yh, HEAD_DIM // 2, 1) * sinv)
                y = jnp.concatenate(heads, axis=1)
            else:
                y = y * jnp.concatenate([factor[dil]] * (cw // LANES), axis=1)
            if kind == "sig":
                y = jax.nn.sigmoid(y + b_ref[:, boff + c0:boff + c0 + cw])
            if tail is not None:
                @pl.when(in_tail)
                def _():
                    if dil == 1:
                        tail[:, c0:c0 + cw] = y
                    else:
                        _store_token_order(tail_scr, y, dil)
                        for j in range(cw // LANES):
                            tail[:, c0 + j * LANES:c0 + (j + 1) * LANES] = tail_scr[j]
            if scale != 1.0:
                y = y * scale
            for r in range(dil):
                out[:, r * width + c0:r * width + c0 + cw] = y[r * per:(r + 1) * per].astype(out.dtype)


def _project(x, gain, cos, sin, segs, *, tm, tail_rows=0, bias=None, wdec=None, bdec=None,
             rolls=()):
    B, L, D = x.shape
    nL = L // tm
    ntail = tail_rows // tm
    first_tail_tile = nL - ntail
    if bias is None:
        bias = jnp.zeros((1, LANES), F32)
    if wdec is None:
        wdec = jnp.zeros((LANES, LANES), BF16)
        bdec = jnp.zeros((1, LANES), F32)

    def const2(shape):
        return pl.BlockSpec(shape, lambda b, i: (0, 0))

    in_specs = [
        pl.BlockSpec((None, tm, D), lambda b, i: (b, i, 0)),
        const2((1, D)),
        pl.BlockSpec((tm, LANES), lambda b, i: (i, 0)),
        pl.BlockSpec((tm, LANES), lambda b, i: (i, 0)),
        const2(bias.shape), const2(wdec.shape), const2(bdec.shape),
    ]
    plan, weights, out_shapes, out_specs = [], [], [], []
    for kind, w, scale, dtype, boff, has_tail, dil in segs:
        width = w.shape[1]
        owidth = wdec.shape[1] if kind == "decay" else width
        plan.append((kind, scale, width, boff, has_tail, dil))
        weights.append(w)
        in_specs.append(const2(w.shape))
        out_shapes.append(jax.ShapeDtypeStruct((B, L // dil, dil * owidth), dtype))
        out_specs.append(pl.BlockSpec((None, tm // dil, dil * owidth), lambda b, i: (b, i, 0)))
        if has_tail:
            out_shapes.append(jax.ShapeDtypeStruct((B, tail_rows, width), F32))
            out_specs.append(pl.BlockSpec(
                (None, tm, width),
                lambda b, i: (b, jnp.maximum(i - first_tail_tile, 0), 0)))
    dils = tuple(sorted({p[5] for p in plan}))
    scratch = []
    if dils != (1,):
        scratch = [pltpu.VMEM((D // LANES + 1, tm, LANES), F32),
                   pltpu.VMEM((COL_CHUNK // LANES, tm, LANES), F32)]
    nsteps = B * nL
    extra, extra_specs, roll_shapes, roll_specs, roll_scratch = _roll_specs(rolls, nsteps)
    body = functools.partial(_proj_body, plan=tuple(plan), first_tail_tile=first_tail_tile,
                             dils=dils)
    body = _with_rolls(body, len(in_specs), len(out_shapes), len(rolls), nsteps,
                       lambda: pl.program_id(0) * nL + pl.program_id(1))
    out = pl.pallas_call(
        body, out_shape=out_shapes + roll_shapes, grid=(B, nL),
        in_specs=in_specs + extra_specs, out_specs=out_specs + roll_specs,
        scratch_shapes=roll_scratch + scratch,
        compiler_params=_params(("arbitrary", "arbitrary")), name="project",
    )(x, gain, cos, sin, bias, wdec, bdec, *weights, *extra)
    return out[:len(out_shapes)], out[len(out_shapes):]


GLA_SUB = 8


def _gla_level_masks(T):
    row = lax.broadcasted_iota(jnp.int32, (T, T), 0)
    col = lax.broadcasted_iota(jnp.int32, (T, T), 1)
    rid = lax.broadcasted_iota(jnp.int32, (T, 1), 0)
    levels = []
    bs = 2 * GLA_SUB
    while bs <= T:
        half = bs // 2
        second = (rid & (bs - 1)) >= half
        pair = ((row & -bs) == (col & -bs)) & ((row & (bs - 1)) >= half) & ((col & (bs - 1)) < half)
        levels.append((second, pair, bs))
        bs *= 2
    return levels


def _gla_matmuls(qf, kf, v, cum, s0, levels):
    T = qf.shape[0]
    last = cum[T - 1:T, :]
    o_inter = _mm((qf * jnp.exp2(cum)).astype(BF16), s0.astype(BF16))
    off = jnp.zeros((T, T), F32)
    for second, pair_mask, bs in levels:
        bound = []
        for j in range(T // bs):
            mid_row = j * bs + bs // 2 - 1
            mid = jnp.broadcast_to(cum[mid_row:mid_row + 1, :], (GLA_SUB, HEAD_DIM))
            bound += [mid] * (bs // GLA_SUB)
        rel = cum - jnp.concatenate(bound, axis=0)
        e = jnp.exp2(jnp.where(second, rel, -rel))
        off = jnp.where(pair_mask, _mm_nt((qf * e).astype(BF16), (kf * e).astype(BF16)), off)
    upd = _mm_tn((kf * jnp.exp2(last - cum)).astype(BF16), v)
    er = lax.broadcasted_iota(jnp.int32, (HEAD_DIM, HEAD_DIM), 0)
    ec = lax.broadcasted_iota(jnp.int32, (HEAD_DIM, HEAD_DIM), 1)
    decay_col = jnp.sum(jnp.where(er == ec, jnp.exp2(last), 0.0), axis=1, keepdims=True)
    return o_inter, off, decay_col * s0 + upd


def _gla_diagonal(qf, kf, cum, off, diag_masks):
    T = qf.shape[0]
    tiles = (T // GLA_SUB, GLA_SUB, HEAD_DIM)
    q3, k3, c3 = qf.reshape(tiles), kf.reshape(tiles), cum.reshape(tiles)
    att3 = off.reshape(T // GLA_SUB, GLA_SUB, T)
    for j in range(GLA_SUB):
        if j == 0:
            prod = q3 * k3
        else:
            prod = q3 * pltpu.roll(k3, j, 1) * jnp.exp2(c3 - pltpu.roll(c3, j, 1))
        att3 = jnp.where(diag_masks[j], jnp.sum(prod, axis=-1, keepdims=True), att3)
    return att3.reshape(T, T)


def _gla_body(q_ref, k_ref, v_ref, lf_ref, r_ref, s0_ref, hn_ref, a_ref, s_ref, *, nb):
    T = q_ref.shape[1]
    hw = HEADS * HEAD_DIM
    first = pl.program_id(1) == 0
    row = lax.broadcasted_iota(jnp.int32, (T, T), 0)
    col = lax.broadcasted_iota(jnp.int32, (T, T), 1)
    tri = jnp.where(row >= col, 1.0, 0.0).astype(BF16)
    levels = _gla_level_masks(T)
    shape3 = (T // GLA_SUB, GLA_SUB, T)
    sub_row = lax.broadcasted_iota(jnp.int32, shape3, 1)
    token = lax.broadcasted_iota(jnp.int32, shape3, 0) * GLA_SUB + sub_row
    key = lax.broadcasted_iota(jnp.int32, shape3, 2)
    diag_masks = [(key == token - j) & (sub_row >= j) for j in range(GLA_SUB)]

    @pl.when(first)
    def _():
        s_ref[...] = s0_ref[...]

    cums = []
    for b in range(nb):
        lf = lf_ref[b] * LOG2_E
        hi = lf.astype(BF16)
        lo = (lf - hi.astype(F32)).astype(BF16)
        parts = _mm(tri, jnp.concatenate([hi, lo], axis=1))
        cums.append(parts[:, :hw] + parts[:, hw:])
    pairs = [(b, h) for b in range(nb) for h in range(HEADS)]
    dk = lambda h: slice(h * HEAD_DIM, (h + 1) * HEAD_DIM)
    dv = lambda h: slice(h * GLA_DV, (h + 1) * GLA_DV)
    qk = {(b, h): (q_ref[b, :, dk(h)].astype(F32), k_ref[b, :, dk(h)].astype(F32), cums[b][:, dk(h)])
          for b, h in pairs}
    stage1 = {}
    for b, h in pairs:
        qf, kf, cum = qk[b, h]
        stage1[b, h] = _gla_matmuls(qf, kf, v_ref[b, :, dv(h)], cum, s_ref[b, h], levels)
        s_ref[b, h] = stage1[b, h][2]
    att = {}
    for b, h in pairs:
        qf, kf, cum = qk[b, h]
        att[b, h] = _gla_diagonal(qf, kf, cum, stage1[b, h][1], diag_masks).astype(BF16)
    for b, h in pairs:
        o = stage1[b, h][0] + _mm(att[b, h], v_ref[b, :, dv(h)])
        r = r_ref[b, :, dv(h)].astype(F32)
        a_ref[b, :, dv(h)] = (_rms(o, hn_ref[...]) * (r * jax.nn.sigmoid(r))).astype(a_ref.dtype)


def _gla(q, k, v, lf, r, s0, head_norm, *, chunk, nb):
    B, L, hw = q.shape
    tok = lambda w: pl.BlockSpec((nb, chunk, w), lambda b, c: (b, c, 0))
    state = pl.BlockSpec((nb, HEADS, HEAD_DIM, GLA_DV), lambda b, c: (b, 0, 0, 0))
    return pl.pallas_call(
        functools.partial(_gla_body, nb=nb),
        out_shape=[jax.ShapeDtypeStruct((B, L, HEADS * GLA_DV), BF16),
                   jax.ShapeDtypeStruct(s0.shape, F32)],
        grid=(B // nb, L // chunk),
        in_specs=[tok(512), tok(512), tok(1024), tok(512), tok(1024), state,
                  pl.BlockSpec((1, GLA_DV), lambda b, c: (0, 0))],
        out_specs=[tok(1024), state],
        compiler_params=_params(("parallel", "arbitrary")), name="gla",
    )(q, k, v, lf, r, s0, head_norm)


def _band_body(q_ref, k_ref, v_ref, kp_ref, vp_ref, o_ref, lse_ref, *, tq):
    step = pl.program_id(2)
    nblk = tq // WINDOW_KEYS
    t = lax.broadcasted_iota(jnp.int32, (WINDOW_KEYS, 2 * WINDOW_KEYS), 0)
    c = lax.broadcasted_iota(jnp.int32, (WINDOW_KEYS, 2 * WINDOW_KEYS), 1)
    band = (c >= t) & (c <= t + WINDOW_KEYS)
    lane = lax.broadcasted_iota(jnp.int32, (WINDOW_KEYS, LANES), 1)
    rows = lambda j: slice(j * WINDOW_KEYS, (j + 1) * WINDOW_KEYS)
    hd = lambda h: slice(h * HEAD_DIM, (h + 1) * HEAD_DIM)

    def window(cur_ref, prev_ref, j, h):
        before = prev_ref[:, hd(h)] if j == 0 else cur_ref[rows(j - 1), hd(h)]
        return jnp.concatenate([before, cur_ref[rows(j), hd(h)]], axis=0)

    pairs = [(j, h) for j in range(nblk) for h in range(HEADS)]
    scores = [_mm_nt(q_ref[rows(j), hd(h)], window(k_ref, kp_ref, j, h)) for j, h in pairs]
    probs = []
    for (j, h), s in zip(pairs, scores):
        first_key = step * tq + (j - 1) * WINDOW_KEYS
        s = jnp.where(band & (c + first_key >= 0), s, NEG)
        m = jnp.max(s, axis=-1, keepdims=True)
        p = jnp.exp(s - m)
        den = jnp.sum(p, axis=-1, keepdims=True)
        probs.append((p.astype(BF16), den, m + jnp.log(den)))
    lse_blk = [jnp.zeros((WINDOW_KEYS, LANES), F32) for _ in range(nblk)]
    for (j, h), (p, den, lse) in zip(pairs, probs):
        o_ref[rows(j), hd(h)] = (_mm(p, window(v_ref, vp_ref, j, h)) / den).astype(o_ref.dtype)
        lse_blk[j] = jnp.where(lane == h, lse, lse_blk[j])
    for j in range(nblk):
        lse_ref[rows(j), :] = lse_blk[j][:, :HEADS]


def _band_attention(q, k, v, dil, *, tq):
    B, n, _ = q.shape
    W = HEADS * HEAD_DIM
    per = tq // WINDOW_KEYS
    cur = pl.BlockSpec((None, tq, W), lambda b, r, i: (b, i, r))
    prv = pl.BlockSpec((None, WINDOW_KEYS, W), lambda b, r, i: (b, jnp.maximum(i * per - 1, 0), r))
    o, lse = pl.pallas_call(
        functools.partial(_band_body, tq=tq),
        out_shape=[jax.ShapeDtypeStruct((B, n, dil * W), BF16),
                   jax.ShapeDtypeStruct((B, dil, n, HEADS), F32)],
        grid=(B, dil, n // tq),
        in_specs=[cur, cur, cur, prv, prv],
        out_specs=[cur, pl.BlockSpec((None, None, tq, HEADS), lambda b, r, i: (b, r, i, 0))],
        compiler_params=_params(("parallel", "parallel", "arbitrary")), name="band_attention",
    )(q, k, v, k, v)
    return o, lse.transpose(0, 2, 1, 3).reshape(B, n * dil, HEADS)


def _cross_body(q_ref, mk_ref, mv_ref, o_ref, *, nb):
    slots = mk_ref.shape[1] // HEADS
    pairs = [(b, h) for b in range(nb) for h in range(HEADS)]
    head_rows = lambda h: pl.ds(h, slots, stride=HEADS)
    lanes = lambda h: slice(h * HEAD_DIM, (h + 1) * HEAD_DIM)
    scores = [_mm_nt(q_ref[b, :, lanes(h)], mk_ref[b, head_rows(h), :].astype(BF16))
              for b, h in pairs]
    probs = []
    for s in scores:
        p = jnp.exp(s - jnp.max(s, axis=-1, keepdims=True))
        probs.append((p.astype(BF16), jnp.sum(p, axis=-1, keepdims=True)))
    for (b, h), (p, den) in zip(pairs, probs):
        pv = _mm(p, mv_ref[b, head_rows(h), :].astype(BF16))
        o_ref[b, :, lanes(h)] = (pv / den).astype(o_ref.dtype)


def _cross_attention(q, mk, mv, *, tq, nb):
    B, L, W = q.shape
    tok = pl.BlockSpec((nb, tq, W), lambda b, i: (b, i, 0))
    mem = pl.BlockSpec((nb,) + mk.shape[1:], lambda b, i: (b, 0, 0))
    return pl.pallas_call(
        functools.partial(_cross_body, nb=nb), out_shape=jax.ShapeDtypeStruct((B, L, W), BF16),
        grid=(B // nb, L // tq), in_specs=[tok, mem, mem], out_specs=tok,
        compiler_params=_params(("parallel", "parallel")), name="cross_attention",
    )(q, mk, mv)


DECODE_ROW_GROUP = 16


DECODE_ROWS = 16


def _decode_body(q_ref, kn_ref, vn_ref, ck_ref, cv_ref, o_ref, lse_ref, *, dil, nnew, nb):
    grouped = len(ck_ref.shape) == 4
    nk = (nnew if grouped else dil) * WINDOW_KEYS
    qrow = lax.broadcasted_iota(jnp.int32, (DECODE_ROWS, nk), 0)
    kcol = lax.broadcasted_iota(jnp.int32, (DECODE_ROWS, nk), 1)
    if dil == 1:
        visible = kcol >= qrow
    elif grouped:
        visible = (kcol // WINDOW_KEYS) == qrow
    else:
        visible = (kcol % dil) == qrow
    new_q = lax.broadcasted_iota(jnp.int32, (nnew, 1), 0)
    head_new = lambda h: pl.ds(h, nnew, stride=HEADS)

    def cached(ref, b, h):
        if grouped:
            return jnp.concatenate([ref[b, :, c * HEADS + h, :] for c in range(nnew)], axis=0)
        return ref[b, pl.ds(h, nk, stride=HEADS), :]

    pairs = [(b, h) for b in range(nb) for h in range(HEADS)]
    pad = jnp.zeros((DECODE_ROWS - nnew, LANES), F32)
    queries = {bh: q_ref[bh[0], head_new(bh[1]), :] for bh in pairs}
    scores = {bh: _mm_nt(jnp.concatenate([queries[bh], pad], axis=0).astype(BF16),
                         cached(ck_ref, *bh).astype(BF16)) for bh in pairs}
    soft = {}
    for b, h in pairs:
        s = jnp.where(visible, scores[b, h], NEG)[:nnew]
        kn = kn_ref[b, head_new(h), :]
        sn = []
        for c in range(nnew):
            ok = (new_q >= c) if dil == 1 else (new_q == c)
            sn.append(jnp.where(ok, jnp.sum(queries[b, h] * kn[c:c + 1, :], axis=-1, keepdims=True),
                                NEG))
        m = jnp.max(s, axis=-1, keepdims=True)
        for t in sn:
            m = jnp.maximum(m, t)
        p = jnp.exp(s - m)
        pn = [jnp.exp(t - m) for t in sn]
        den = jnp.sum(p, axis=-1, keepdims=True)
        for t in pn:
            den = den + t
        p_rows = jnp.concatenate([p, jnp.zeros((DECODE_ROWS - nnew, nk), F32)], axis=0)
        soft[b, h] = (p_rows.astype(BF16), pn, den, m + jnp.log(den))
    for b, h in pairs:
        p, pn, den, lse = soft[b, h]
        acc = _mm(p, cached(cv_ref, b, h).astype(BF16))[:nnew]
        vn = vn_ref[b, head_new(h), :]
        for c in range(nnew):
            acc = acc + pn[c] * vn[c:c + 1, :]
        o_ref[b, head_new(h), :] = acc / den
        lse_ref[b, head_new(h), :] = jnp.broadcast_to(lse, (nnew, LANES))


def _decode_attention(q, kn, vn, ck, cv, window, dil, *, nb):
    B, nrow, _ = q.shape
    rows = window * HEADS
    small = pl.BlockSpec((nb, nrow, LANES), lambda b: (b, 0, 0))
    slot_rows = HEADS * dil
    if slot_rows > DECODE_ROW_GROUP:
        view = lambda c: c.reshape(B, WINDOW_KEYS, slot_rows, LANES)
        big = pl.BlockSpec((nb, WINDOW_KEYS, DECODE_ROW_GROUP, LANES), lambda b: (b, 0, 0, 0))
    else:
        view = lambda c: c
        big = pl.BlockSpec((nb, rows, LANES), lambda b: (b, 0, 0))
    return pl.pallas_call(
        functools.partial(_decode_body, dil=dil, nnew=nrow // HEADS, nb=nb),
        out_shape=[jax.ShapeDtypeStruct((B, nrow, LANES), F32),
                   jax.ShapeDtypeStruct((B, nrow, LANES), F32)],
        grid=(B // nb,), in_specs=[small, small, small, big, big],
        out_specs=[small, small],
        compiler_params=_params(("parallel",)), name="decode_attention",
    )(q, kn, vn, view(ck), view(cv))


def _roll_copies(step, nsteps, old, new, out, buf, sems):
    total, rows, _ = old.shape
    shift = new.shape[1]
    per = total // nsteps
    batches = pl.ds(step * per, per)
    reads = [
        pltpu.make_async_copy(old.at[batches, pl.ds(shift, rows - shift), :],
                              buf.at[:, pl.ds(0, rows - shift), :], sems.at[0]),
        pltpu.make_async_copy(new.at[batches], buf.at[:, pl.ds(rows - shift, shift), :],
                              sems.at[1]),
    ]
    return reads, pltpu.make_async_copy(buf, out.at[batches], sems.at[2])


def _with_rolls(body, nfixed_in, nfixed_out, nrolls, nsteps, step_fn):
    if nrolls == 0:
        return body

    def wrapped(*refs):
        ins = refs[:nfixed_in]
        pairs = refs[nfixed_in:nfixed_in + 2 * nrolls]
        base = nfixed_in + 2 * nrolls
        outs = refs[base:base + nfixed_out]
        rolled = refs[base + nfixed_out:base + nfixed_out + nrolls]
        staging = refs[base + nfixed_out + nrolls:base + nfixed_out + 3 * nrolls]
        rest = refs[base + nfixed_out + 3 * nrolls:]
        step = step_fn()
        args = [(pairs[2 * t], pairs[2 * t + 1], rolled[t], staging[2 * t + 1], staging[2 * t])
                for t in range(nrolls)]

        @pl.when(step > 0)
        def _():
            for a in args:
                _roll_copies(step - 1, nsteps, *a)[1].wait()

        current = [_roll_copies(step, nsteps, *a) for a in args]
        for reads, _ in current:
            for c in reads:
                c.start()
        body(*ins, *outs, *rest)
        for reads, write in current:
            for c in reads:
                c.wait()
            write.start()

        @pl.when(step == nsteps - 1)
        def _():
            for _, write in current:
                write.wait()

    return wrapped


def _roll_specs(rolls, nsteps):
    any_spec = pl.BlockSpec(memory_space=pl.ANY)
    extra, shapes, scratch = [], [], []
    for old, new in rolls:
        assert old.shape[0] % nsteps == 0
        extra += [old, new]
        shapes.append(jax.ShapeDtypeStruct(old.shape, old.dtype))
        scratch += [pltpu.SemaphoreType.DMA((3,)),
                    pltpu.VMEM((old.shape[0] // nsteps,) + old.shape[1:], old.dtype)]
    return extra, [any_spec] * len(extra), shapes, [any_spec] * len(shapes), scratch


def _merge_body(x_ref, a_ref, o1_ref, o2_ref, o3_ref, l1_ref, l2_ref, l3_ref, ox_ref, g_ref,
                wa_ref, wb_ref, wc_ref, wo_ref, h_ref, *scratch, dils):
    tm = x_ref.shape[0]
    hw = HEADS * HEAD_DIM
    groups, si = [], 0
    for o_ref, d in zip((o1_ref, o2_ref, o3_ref), dils):
        if d == 1:
            groups.append([o_ref[:, h * HEAD_DIM:(h + 1) * HEAD_DIM] for h in range(HEADS)])
            continue
        buf = scratch[si]
        si += 1
        for r in range(d):
            for h in range(HEADS):
                lanes = slice(r * hw + h * HEAD_DIM, r * hw + (h + 1) * HEAD_DIM)
                buf[h, pl.ds(r, tm // d, stride=d), :] = o_ref[:, lanes].astype(F32)
        groups.append([buf[h] for h in range(HEADS)])
    l1, l2, l3 = l1_ref[...], l2_ref[...], l3_ref[...]
    lmax = jnp.maximum(jnp.maximum(l1, l2), l3)
    e1, e2, e3 = jnp.exp(l1 - lmax), jnp.exp(l2 - lmax), jnp.exp(l3 - lmax)
    inv = 1.0 / (e1 + e2 + e3)
    heads = []
    for h in range(HEADS):
        heads.append((e1[:, h:h + 1] * inv[:, h:h + 1]) * groups[0][h].astype(F32)
                     + (e2[:, h:h + 1] * inv[:, h:h + 1]) * groups[1][h].astype(F32)
                     + (e3[:, h:h + 1] * inv[:, h:h + 1]) * groups[2][h].astype(F32))
    o_dil = jnp.concatenate(heads, axis=1).astype(BF16)
    ya = _mm(a_ref[...], wa_ref[...])
    yb = _mm(o_dil, wb_ref[...])
    yc = _mm(ox_ref[...], wc_ref[...])
    d = D_MODEL
    mix = (g_ref[:, 0:d].astype(F32) * ya + g_ref[:, d:2 * d].astype(F32) * yb
           + g_ref[:, 2 * d:3 * d].astype(F32) * yc)
    h_ref[...] = x_ref[...] + _mm(mix.astype(BF16), wo_ref[...])


def _merge(x, a, o_groups, lse_groups, ox, gates, wa, wb, wc, wo, *, tm, dils, rolls=()):
    T = x.shape[0]
    nsteps = T // tm
    hw = HEADS * HEAD_DIM
    tok = lambda w: pl.BlockSpec((tm, w), lambda i: (i, 0))
    grp = lambda d: pl.BlockSpec((tm // d, d * hw), lambda i: (i, 0))
    full = lambda w: pl.BlockSpec(w.shape, lambda i: (0, 0))
    extra, extra_specs, roll_shapes, roll_specs, roll_scratch = _roll_specs(rolls, nsteps)
    body = _with_rolls(functools.partial(_merge_body, dils=dils), 14, 1, len(rolls), nsteps,
                       lambda: pl.program_id(0))
    out = pl.pallas_call(
        body, out_shape=[jax.ShapeDtypeStruct((T, D_MODEL), F32)] + roll_shapes, grid=(nsteps,),
        in_specs=[tok(D_MODEL), tok(HEADS * GLA_DV), grp(dils[0]), grp(dils[1]), grp(dils[2]),
                  tok(HEADS), tok(HEADS), tok(HEADS), tok(hw), tok(3 * D_MODEL),
                  full(wa), full(wb), full(wc), full(wo)] + extra_specs,
        out_specs=[tok(D_MODEL)] + roll_specs,
        scratch_shapes=roll_scratch + [pltpu.VMEM((HEADS, tm, HEAD_DIM), F32)
                                       for d in dils if d > 1],
        compiler_params=_params(("arbitrary",)), name="merge",
    )(x, a, *o_groups, *lse_groups, ox, gates, wa, wb, wc, wo, *extra)
    return out[0], out[1:]


def _ffn_body(h_ref, gn_ref, wu_ref, wd_ref, gf_ref, y_ref):
    h = h_ref[...]
    n = (h * gn_ref[...]).astype(BF16)
    inv_ms = 1.0 / (jnp.mean(h * h, axis=-1, keepdims=True) + EPS)
    acc = None
    for c0 in range(0, D_FF, COL_CHUNK):
        u = jnp.maximum(_mm(n, wu_ref[:, c0:c0 + COL_CHUNK]), 0.0)
        part = _mm((u * u).astype(BF16), wd_ref[c0:c0 + COL_CHUNK, :])
        acc = part if acc is None else acc + part
    y_ref[...] = _rms(h + inv_ms * acc, gf_ref[...])


def _ffn(h, gain_ffn, wu, wd, gain_final, *, tm, rolls=()):
    T = h.shape[0]
    nsteps = T // tm
    tok = pl.BlockSpec((tm, D_MODEL), lambda i: (i, 0))
    full = lambda w: pl.BlockSpec(w.shape, lambda i: (0, 0), pipeline_mode=pl.Buffered(1))
    extra, extra_specs, roll_shapes, roll_specs, roll_scratch = _roll_specs(rolls, nsteps)
    body = _with_rolls(_ffn_body, 5, 1, len(rolls), nsteps, lambda: pl.program_id(0))
    out = pl.pallas_call(
        body, out_shape=[jax.ShapeDtypeStruct((T, D_MODEL), F32)] + roll_shapes, grid=(nsteps,),
        in_specs=[tok, full(gain_ffn), full(wu), full(wd), full(gain_final)] + extra_specs,
        out_specs=[tok] + roll_specs, scratch_shapes=roll_scratch,
        compiler_params=_params(("arbitrary",)), name="ffn",
    )(h, gain_ffn, wu, wd, gain_final, *extra)
    return out[0], out[1:]


def _rope_tables(pos):
    half = HEAD_DIM // 2
    inv = ROPE_THETA ** (-jnp.arange(half, dtype=F32) / half)
    ang = pos.astype(F32)[:, None] * inv[None, :]
    cos, sin = jnp.cos(ang), jnp.sin(ang)
    return jnp.concatenate([cos, cos], axis=-1), jnp.concatenate([-sin, sin], axis=-1)


def _split_weights(w_in, w_decay):
    hw = HEADS * HEAD_DIM
    sizes = [hw, hw, HEADS * GLA_DV, HEADS * GLA_DV, GLA_RANK] + [hw] * 9 + [hw, 3 * D_MODEL]
    offs = [0]
    for s in sizes:
        offs.append(offs[-1] + s)
    cols = [w_in[:, offs[j]:offs[j + 1]].astype(BF16) for j in range(len(sizes))]
    cols[4] = jnp.pad(cols[4], ((0, 0), (0, LANES - GLA_RANK)))
    wdec = jnp.pad(w_decay.astype(BF16), ((0, LANES - GLA_RANK), (0, 0)))
    return cols, wdec


def _layer(x, pos, tail_rows, cols, wdec, b_decay, b_gate, norm_mix, tm, dils, rolls=()):
    cos, sin = _rope_tables(pos)
    gain = norm_mix[None, :]
    (gq, gk, gv, gr, lf), _ = _project(
        x, gain, cos, sin,
        [("lin", cols[0], QK_SCALE, BF16, 0, False, 1), ("lin", cols[1], 1.0, BF16, 0, False, 1),
         ("lin", cols[2], 1.0, BF16, 0, False, 1), ("lin", cols[3], 1.0, BF16, 0, False, 1),
         ("decay", cols[4], 1.0, F32, 0, False, 1)],
        tm=tm, wdec=wdec, bdec=b_decay[None, :])
    dil_segs = []
    for g in range(3):
        dil_segs += [("rope", cols[5 + 3 * g], QK_SCALE, BF16, 0, False, dils[g]),
                     ("rope", cols[6 + 3 * g], 1.0, BF16, 0, True, dils[g]),
                     ("lin", cols[7 + 3 * g], 1.0, BF16, 0, True, dils[g])]
    dil_out, _ = _project(x, gain, cos, sin, dil_segs, tm=tm, tail_rows=tail_rows)
    (xq, gates), rolled = _project(
        x, gain, cos, sin,
        [("lin", cols[14], QK_SCALE, BF16, 0, False, 1), ("sig", cols[15], 1.0, BF16, 0, False, 1)],
        tm=tm, bias=b_gate[None, :], rolls=rolls)
    dil = [dil_out[5 * g:5 * g + 5] for g in range(3)]
    return (gq, gk, gv, gr, lf), dil, xq, gates, rolled


def _finish(x2d, a, o_groups, lse_groups, ox, gates, post, tm, dils, merge_rolls=(),
            ffn_rolls=()):
    wa, wb, wc, wo, gain_ffn, wu, wd, gain_final = post
    flat = lambda t: t.reshape(-1, t.shape[-1])
    h, rolled_m = _merge(x2d, flat(a), [flat(t) for t in o_groups], [flat(t) for t in lse_groups],
                         flat(ox), flat(gates), wa, wb, wc, wo, tm=tm, dils=dils,
                         rolls=merge_rolls)
    y, rolled_f = _ffn(h, gain_ffn, wu, wd, gain_final, tm=tm, rolls=ffn_rolls)
    return y, rolled_m, rolled_f


def kernel(x_prompt, x_sample, mem_prompt, state_gla, cache_dil1_k, cache_dil1_v, cache_dil2_k, cache_dil2_v, cache_dil3_k, cache_dil3_v, cache_mem_k, cache_mem_v, norm_mix, w_in, b_gate, w_decay, b_decay, gla_head_norm, w_proj_gla, w_proj_dil, w_proj_x, norm_mem, w_mem_kv, w_out, norm_ffn, w_ffn_up, w_ffn_down, norm_final):
    B, L, D = x_prompt.shape
    SB, SL, _ = x_sample.shape
    depth = w_in.shape[0]
    assert depth == 1, "single trunk layer"
    i = 0
    cols, wdec = _split_weights(w_in[i], w_decay[i])
    post = (w_proj_gla[i].astype(BF16), w_proj_dil[i].astype(BF16), w_proj_x[i].astype(BF16),
            w_out[i].astype(BF16), norm_ffn[i][None, :], w_ffn_up[i].astype(BF16),
            w_ffn_down[i].astype(BF16), norm_final[None, :])
    head_norm = gla_head_norm[i][None, :]
    hw = HEADS * HEAD_DIM
    max_window = DIL_PATTERNS[-1][0]

    mem_rows = lambda t: t.reshape(t.shape[0], N_MEM * HEADS, HEAD_DIM)

    T = SB * SL
    pos_s = jnp.tile(PAST_LEN + jnp.arange(SL), SB)
    (gq, gk, gv, gr, lf), dil, xq, gates, _ = _layer(
        x_sample.reshape(1, T, D), pos_s, T, cols, wdec, b_decay[i], b_gate[i], norm_mix[i], T,
        (1, 1, 1))
    pad_rows = 16 - SL
    per_batch = lambda t: jnp.pad(t.reshape(SB, SL, t.shape[-1]), ((0, 0), (0, pad_rows), (0, 0)))
    a_s, state_s = _gla(per_batch(gq), per_batch(gk), per_batch(gv), per_batch(lf), per_batch(gr),
                        state_gla[i], head_norm, chunk=16, nb=8)
    a_s = a_s[:, :SL]
    caches = ((cache_dil1_k[i], cache_dil1_v[i]), (cache_dil2_k[i], cache_dil2_v[i]),
              (cache_dil3_k[i], cache_dil3_v[i]))
    o_groups, lse_groups, roll_pairs = [], [], []
    rows = lambda t: t.astype(F32).reshape(SB, SL * HEADS, HEAD_DIM)
    for (win, dl), (dq, dk, dk_tail, dv, dv_tail), (ck, cv), nb in zip(
            DIL_PATTERNS, dil, caches, (4, 2, 2)):
        ck, cv = (c.reshape(SB, win * HEADS, HEAD_DIM) for c in (ck, cv))
        o_g, lse_g = _decode_attention(rows(dq), rows(dk_tail), rows(dv_tail), ck, cv, win, dl,
                                       nb=nb)
        o_groups.append(o_g.reshape(SB, SL, hw).astype(BF16))
        lse_groups.append(lse_g[:, :, 0].reshape(SB, SL, HEADS))
        roll_pairs += [(ck, rows(dk_tail)), (cv, rows(dv_tail))]
    ox = _cross_attention(per_batch(xq), mem_rows(cache_mem_k[i]), mem_rows(cache_mem_v[i]),
                          tq=16, nb=8)[:, :SL]
    y_sample, _, _ = _finish(x_sample.reshape(T, D), a_s, o_groups, lse_groups, ox, gates, post,
                             T, (1, 1, 1))
    y_sample = y_sample.reshape(SB, SL, D)

    dils_p = tuple(dl for _, dl in DIL_PATTERNS)
    (gq, gk, gv, gr, lf), dil, xq, gates, rolled_v3 = _layer(
        x_prompt, jnp.arange(L), max_window, cols, wdec, b_decay[i], b_gate[i], norm_mix[i], 512,
        dils_p, rolls=roll_pairs[5:6])
    a_p, state_p = _gla(gq, gk, gv, lf, gr, jnp.zeros((B, HEADS, HEAD_DIM, GLA_DV), F32),
                        head_norm, chunk=128, nb=1)
    o_groups, lse_groups, bufs_p = [], [], []
    for (win, dl), (dq, dk, dk_tail, dv, dv_tail) in zip(DIL_PATTERNS, dil):
        o_g, lse_g = _band_attention(dq, dk, dv, dl, tq=512)
        o_groups.append(o_g)
        lse_groups.append(lse_g)
        keep = min(win, L)
        bufs_p += [dk_tail[:, max_window - keep:].reshape(1, B, keep, HEADS, HEAD_DIM),
                   dv_tail[:, max_window - keep:].reshape(1, B, keep, HEADS, HEAD_DIM)]
    zeros_tab = jnp.zeros((N_MEM, LANES), F32)
    w_mem = w_mem_kv[i].astype(BF16)
    (mk, mv), _ = _project(
        mem_prompt, norm_mem[i][None, :], zeros_tab, zeros_tab,
        [("lin", w_mem[:, :hw], 1.0, F32, 0, False, 1), ("lin", w_mem[:, hw:], 1.0, F32, 0, False, 1)],
        tm=N_MEM)
    ox = _cross_attention(xq, mem_rows(mk), mem_rows(mv), tq=512, nb=1)
    y_prompt, rolled_12, rolled_k3 = _finish(
        x_prompt.reshape(B * L, D), a_p, o_groups, lse_groups, ox, gates, post, 512, dils_p,
        merge_rolls=roll_pairs[:4], ffn_rolls=roll_pairs[4:5])
    y_prompt = y_prompt.reshape(B, L, D)
    bufs_s = [c.reshape(1, SB, c.shape[1] // HEADS, HEADS, HEAD_DIM)
              for c in (*rolled_12, *rolled_k3, *rolled_v3)]

    return (y_prompt, y_sample, state_p[None], *bufs_p,
            mk.reshape(1, B, N_MEM, HEADS, HEAD_DIM), mv.reshape(1, B, N_MEM, HEADS, HEAD_DIM),
            state_s[None], *bufs_s)
```

```python
import functools

import jax
import jax.numpy as jnp
from jax import lax
from jax.experimental import pallas as pl
from jax.experimental.pallas import tpu as pltpu

F32 = jnp.float32
BF16 = jnp.bfloat16

D_MODEL = 1024
PAST_LEN = 8192
N_MEM = 256
HEADS = 4
HEAD_DIM = 128
GLA_DV = 256
GLA_RANK = 16
GLA_GATE_NORM = 16.0
DIL_PATTERNS = ((128, 1), (512, 4), (2048, 16))
WINDOW_KEYS = 128
D_FF = 4 * D_MODEL
ROPE_THETA = 10000.0
EPS = 1e-6
NEG = -1e30
QK_SCALE = HEAD_DIM ** -0.5
LOG2_E = 1.4426950408889634

LANES = 128
COL_CHUNK = 512
VMEM_LIMIT = 48 * 1024 * 1024


def _mm(a, b):
    return jnp.dot(a, b, preferred_element_type=F32)


def _mm_nt(a, b):
    return lax.dot_general(a, b, (((1,), (1,)), ((), ())), preferred_element_type=F32)


def _mm_tn(a, b):
    return lax.dot_general(a, b, (((0,), (0,)), ((), ())), preferred_element_type=F32)


def _rms(x, gain):
    return x * lax.rsqrt(jnp.mean(x * x, axis=-1, keepdims=True) + EPS) * gain


def _params(sem):
    return pltpu.CompilerParams(dimension_semantics=sem, vmem_limit_bytes=VMEM_LIMIT)


def _lane_blocks(y):
    return [y[:, j * LANES:(j + 1) * LANES] for j in range(y.shape[1] // LANES)]


def _residue_major(ref, dil):
    nblk, tm, _ = ref.shape
    per = tm // dil
    return jnp.concatenate(
        [jnp.concatenate([ref[j, pl.ds(r, per, stride=dil), :] for j in range(nblk)], axis=1)
         for r in range(dil)], axis=0)


def _store_token_order(buf, y, dil):
    per = y.shape[0] // dil
    for r in range(dil):
        for j, blk in enumerate(_lane_blocks(y[r * per:(r + 1) * per])):
            buf[j, pl.ds(r, per, stride=dil), :] = blk


def _proj_body(*refs, plan, first_tail_tile, dils):
    x_ref, g_ref, cos_ref, sin_ref, b_ref, wdec_ref, bdec_ref = refs[:7]
    w_refs = refs[7:7 + len(plan)]
    nout = sum(2 if p[4] else 1 for p in plan)
    out_refs = list(refs[7 + len(plan):7 + len(plan) + nout])
    scratch = refs[7 + len(plan) + nout:]
    tm, dm = x_ref.shape
    x = x_ref[...]
    xg = x * g_ref[...]
    inv_rms = jnp.broadcast_to(lax.rsqrt(jnp.mean(x * x, axis=-1, keepdims=True) + EPS),
                               (tm, LANES))
    if scratch:
        n_scr, tail_scr = scratch
        for j, blk in enumerate(_lane_blocks(xg) + [inv_rms]):
            n_scr[j] = blk
    rows_of = {1: xg.astype(BF16)}
    factor = {1: inv_rms}
    tables = {1: (cos_ref[...] * inv_rms, sin_ref[...] * inv_rms)}
    for d in dils:
        if d > 1:
            permuted = _residue_major(n_scr, d)
            rows_of[d] = permuted[:, :dm].astype(BF16)
            factor[d] = permuted[:, dm:]
            tables[d] = tuple(
                jnp.concatenate([t[pl.ds(r, tm // d, stride=d), :] for r in range(d)], axis=0)
                * factor[d] for t in (cos_ref, sin_ref))
    in_tail = pl.program_id(1) >= first_tail_tile
    oi = 0
    for (kind, scale, width, boff, has_tail, dil), w_ref in zip(plan, w_refs):
        out = out_refs[oi]
        oi += 1
        tail = None
        if has_tail:
            tail = out_refs[oi]
            oi += 1
        n = rows_of[dil]
        if kind == "decay":
            ga = (_mm(n, w_ref[...]) * factor[dil]).astype(BF16)
            z = _mm(ga, wdec_ref[...]) + bdec_ref[...]
            logsig = jnp.minimum(z, 0.0) - jnp.log1p(jnp.exp(-jnp.abs(z)))
            out[...] = logsig * (1.0 / GLA_GATE_NORM)
            continue
        per = tm // dil
        for c0 in range(0, width, COL_CHUNK):
            cw = min(COL_CHUNK, width - c0)
            y = _mm(n, w_ref[:, c0:c0 + cw])
            if kind == "rope":
                cosv, sinv = tables[dil]
                heads = []
                for j in range(cw // HEAD_DIM):
                    yh = y[:, j * HEAD_DIM:(j + 1) * HEAD_DIM]
                    heads.append(yh * cosv + pltpu.roll(yh, HEAD_DIM // 2, 1) * sinv)
                y = jnp.concatenate(heads, axis=1)
            else:
                y = y * jnp.concatenate([factor[dil]] * (cw // LANES), axis=1)
            if kind == "sig":
                y = jax.nn.sigmoid(y + b_ref[:, boff + c0:boff + c0 + cw])
            if tail is not None:
                @pl.when(in_tail)
                def _():
                    if dil == 1:
                        tail[:, c0:c0 + cw] = y
                    else:
                        _store_token_order(tail_scr, y, dil)
                        for j in range(cw // LANES):
                            tail[:, c0 + j * LANES:c0 + (j + 1) * LANES] = tail_scr[j]
            if scale != 1.0:
                y = y * scale
            for r in range(dil):
                out[:, r * width + c0:r * width + c0 + cw] = y[r * per:(r + 1) * per].astype(out.dtype)


def _project(x, gain, cos, sin, segs, *, tm, tail_rows=0, bias=None, wdec=None, bdec=None,
             rolls=()):
    B, L, D = x.shape
    nL = L // tm
    ntail = tail_rows // tm
    first_tail_tile = nL - ntail
    if bias is None:
        bias = jnp.zeros((1, LANES), F32)
    if wdec is None:
        wdec = jnp.zeros((LANES, LANES), BF16)
        bdec = jnp.zeros((1, LANES), F32)

    def const2(shape):
        return pl.BlockSpec(shape, lambda b, i: (0, 0))

    in_specs = [
        pl.BlockSpec((None, tm, D), lambda b, i: (b, i, 0)),
        const2((1, D)),
        pl.BlockSpec((tm, LANES), lambda b, i: (i, 0)),
        pl.BlockSpec((tm, LANES), lambda b, i: (i, 0)),
        const2(bias.shape), const2(wdec.shape), const2(bdec.shape),
    ]
    plan, weights, out_shapes, out_specs = [], [], [], []
    for kind, w, scale, dtype, boff, has_tail, dil in segs:
        width = w.shape[1]
        owidth = wdec.shape[1] if kind == "decay" else width
        plan.append((kind, scale, width, boff, has_tail, dil))
        weights.append(w)
        in_specs.append(const2(w.shape))
        out_shapes.append(jax.ShapeDtypeStruct((B, L // dil, dil * owidth), dtype))
        out_specs.append(pl.BlockSpec((None, tm // dil, dil * owidth), lambda b, i: (b, i, 0)))
        if has_tail:
            out_shapes.append(jax.ShapeDtypeStruct((B, tail_rows, width), F32))
            out_specs.append(pl.BlockSpec(
                (None, tm, width),
                lambda b, i: (b, jnp.maximum(i - first_tail_tile, 0), 0)))
    dils = tuple(sorted({p[5] for p in plan}))
    scratch = []
    if dils != (1,):
        scratch = [pltpu.VMEM((D // LANES + 1, tm, LANES), F32),
                   pltpu.VMEM((COL_CHUNK // LANES, tm, LANES), F32)]
    nsteps = B * nL
    extra, extra_specs, roll_shapes, roll_specs, roll_scratch = _roll_specs(rolls, nsteps)
    body = functools.partial(_proj_body, plan=tuple(plan), first_tail_tile=first_tail_tile,
                             dils=dils)
    body = _with_rolls(body, len(in_specs), len(out_shapes), len(rolls), nsteps,
                       lambda: pl.program_id(0) * nL + pl.program_id(1))
    out = pl.pallas_call(
        body, out_shape=out_shapes + roll_shapes, grid=(B, nL),
        in_specs=in_specs + extra_specs, out_specs=out_specs + roll_specs,
        scratch_shapes=roll_scratch + scratch,
        compiler_params=_params(("arbitrary", "arbitrary")), name="project",
    )(x, gain, cos, sin, bias, wdec, bdec, *weights, *extra)
    return out[:len(out_shapes)], out[len(out_shapes):]


GLA_SUB = 8


def _gla_level_masks(T):
    row = lax.broadcasted_iota(jnp.int32, (T, T), 0)
    col = lax.broadcasted_iota(jnp.int32, (T, T), 1)
    rid = lax.broadcasted_iota(jnp.int32, (T, 1), 0)
    levels = []
    bs = 2 * GLA_SUB
    while bs <= T:
        half = bs // 2
        second = (rid & (bs - 1)) >= half
        pair = ((row & -bs) == (col & -bs)) & ((row & (bs - 1)) >= half) & ((col & (bs - 1)) < half)
        levels.append((second, pair, bs))
        bs *= 2
    return levels


def _gla_matmuls(qf, kf, v, cum, s0, levels):
    T = qf.shape[0]
    last = cum[T - 1:T, :]
    o_inter = _mm((qf * jnp.exp2(cum)).astype(BF16), s0.astype(BF16))
    off = jnp.zeros((T, T), F32)
    for second, pair_mask, bs in levels:
        bound = []
        for j in range(T // bs):
            mid_row = j * bs + bs // 2 - 1
            mid = jnp.broadcast_to(cum[mid_row:mid_row + 1, :], (GLA_SUB, HEAD_DIM))
            bound += [mid] * (bs // GLA_SUB)
        rel = cum - jnp.concatenate(bound, axis=0)
        e = jnp.exp2(jnp.where(second, rel, -rel))
        off = jnp.where(pair_mask, _mm_nt((qf * e).astype(BF16), (kf * e).astype(BF16)), off)
    upd = _mm_tn((kf * jnp.exp2(last - cum)).astype(BF16), v)
    er = lax.broadcasted_iota(jnp.int32, (HEAD_DIM, HEAD_DIM), 0)
    ec = lax.broadcasted_iota(jnp.int32, (HEAD_DIM, HEAD_DIM), 1)
    decay_col = jnp.sum(jnp.where(er == ec, jnp.exp2(last), 0.0), axis=1, keepdims=True)
    return o_inter, off, decay_col * s0 + upd


def _gla_diagonal(qf, kf, cum, off, diag_masks):
    T = qf.shape[0]
    tiles = (T // GLA_SUB, GLA_SUB, HEAD_DIM)
    q3, k3, c3 = qf.reshape(tiles), kf.reshape(tiles), cum.reshape(tiles)
    att3 = off.reshape(T // GLA_SUB, GLA_SUB, T)
    for j in range(GLA_SUB):
        if j == 0:
            prod = q3 * k3
        else:
            prod = q3 * pltpu.roll(k3, j, 1) * jnp.exp2(c3 - pltpu.roll(c3, j, 1))
        att3 = jnp.where(diag_masks[j], jnp.sum(prod, axis=-1, keepdims=True), att3)
    return att3.reshape(T, T)


def _gla_body(q_ref, k_ref, v_ref, lf_ref, r_ref, s0_ref, hn_ref, a_ref, s_ref, *, nb):
    T = q_ref.shape[1]
    hw = HEADS * HEAD_DIM
    first = pl.program_id(1) == 0
    row = lax.broadcasted_iota(jnp.int32, (T, T), 0)
    col = lax.broadcasted_iota(jnp.int32, (T, T), 1)
    tri = jnp.where(row >= col, 1.0, 0.0).astype(BF16)
    levels = _gla_level_masks(T)
    shape3 = (T // GLA_SUB, GLA_SUB, T)
    sub_row = lax.broadcasted_iota(jnp.int32, shape3, 1)
    token = lax.broadcasted_iota(jnp.int32, shape3, 0) * GLA_SUB + sub_row
    key = lax.broadcasted_iota(jnp.int32, shape3, 2)
    diag_masks = [(key == token - j) & (sub_row >= j) for j in range(GLA_SUB)]

    @pl.when(first)
    def _():
        s_ref[...] = s0_ref[...]

    cums = []
    for b in range(nb):
        lf = lf_ref[b] * LOG2_E
        hi = lf.astype(BF16)
        lo = (lf - hi.astype(F32)).astype(BF16)
        parts = _mm(tri, jnp.concatenate([hi, lo], axis=1))
        cums.append(parts[:, :hw] + parts[:, hw:])
    pairs = [(b, h) for b in range(nb) for h in range(HEADS)]
    dk = lambda h: slice(h * HEAD_DIM, (h + 1) * HEAD_DIM)
    dv = lambda h: slice(h * GLA_DV, (h + 1) * GLA_DV)
    qk = {(b, h): (q_ref[b, :, dk(h)].astype(F32), k_ref[b, :, dk(h)].astype(F32), cums[b][:, dk(h)])
          for b, h in pairs}
    stage1 = {}
    for b, h in pairs:
        qf, kf, cum = qk[b, h]
        stage1[b, h] = _gla_matmuls(qf, kf, v_ref[b, :, dv(h)], cum, s_ref[b, h], levels)
        s_ref[b, h] = stage1[b, h][2]
    att = {}
    for b, h in pairs:
        qf, kf, cum = qk[b, h]
        att[b, h] = _gla_diagonal(qf, kf, cum, stage1[b, h][1], diag_masks).astype(BF16)
    for b, h in pairs:
        o = stage1[b, h][0] + _mm(att[b, h], v_ref[b, :, dv(h)])
        r = r_ref[b, :, dv(h)].astype(F32)
        a_ref[b, :, dv(h)] = (_rms(o, hn_ref[...]) * (r * jax.nn.sigmoid(r))).astype(a_ref.dtype)


def _gla(q, k, v, lf, r, s0, head_norm, *, chunk, nb):
    B, L, hw = q.shape
    tok = lambda w: pl.BlockSpec((nb, chunk, w), lambda b, c: (b, c, 0))
    state = pl.BlockSpec((nb, HEADS, HEAD_DIM, GLA_DV), lambda b, c: (b, 0, 0, 0))
    return pl.pallas_call(
        functools.partial(_gla_body, nb=nb),
        out_shape=[jax.ShapeDtypeStruct((B, L, HEADS * GLA_DV), BF16),
                   jax.ShapeDtypeStruct(s0.shape, F32)],
        grid=(B // nb, L // chunk),
        in_specs=[tok(512), tok(512), tok(1024), tok(512), tok(1024), state,
                  pl.BlockSpec((1, GLA_DV), lambda b, c: (0, 0))],
        out_specs=[tok(1024), state],
        compiler_params=_params(("parallel", "arbitrary")), name="gla",
    )(q, k, v, lf, r, s0, head_norm)


def _band_body(q_ref, k_ref, v_ref, kp_ref, vp_ref, o_ref, lse_ref, *, tq):
    step = pl.program_id(2)
    nblk = tq // WINDOW_KEYS
    t = lax.broadcasted_iota(jnp.int32, (WINDOW_KEYS, 2 * WINDOW_KEYS), 0)
    c = lax.broadcasted_iota(jnp.int32, (WINDOW_KEYS, 2 * WINDOW_KEYS), 1)
    band = (c >= t) & (c <= t + WINDOW_KEYS)
    lane = lax.broadcasted_iota(jnp.int32, (WINDOW_KEYS, LANES), 1)
    rows = lambda j: slice(j * WINDOW_KEYS, (j + 1) * WINDOW_KEYS)
    hd = lambda h: slice(h * HEAD_DIM, (h + 1) * HEAD_DIM)

    def window(cur_ref, prev_ref, j, h):
        before = prev_ref[:, hd(h)] if j == 0 else cur_ref[rows(j - 1), hd(h)]
        return jnp.concatenate([before, cur_ref[rows(j), hd(h)]], axis=0)

    pairs = [(j, h) for j in range(nblk) for h in range(HEADS)]
    scores = [_mm_nt(q_ref[rows(j), hd(h)], window(k_ref, kp_ref, j, h)) for j, h in pairs]
    probs = []
    for (j, h), s in zip(pairs, scores):
        first_key = step * tq + (j - 1) * WINDOW_KEYS
        s = jnp.where(band & (c + first_key >= 0), s, NEG)
        m = jnp.max(s, axis=-1, keepdims=True)
        p = jnp.exp(s - m)
        den = jnp.sum(p, axis=-1, keepdims=True)
        probs.append((p.astype(BF16), den, m + jnp.log(den)))
    lse_blk = [jnp.zeros((WINDOW_KEYS, LANES), F32) for _ in range(nblk)]
    for (j, h), (p, den, lse) in zip(pairs, probs):
        o_ref[rows(j), hd(h)] = (_mm(p, window(v_ref, vp_ref, j, h)) / den).astype(o_ref.dtype)
        lse_blk[j] = jnp.where(lane == h, lse, lse_blk[j])
    for j in range(nblk):
        lse_ref[rows(j), :] = lse_blk[j][:, :HEADS]


def _band_attention(q, k, v, dil, *, tq):
    B, n, _ = q.shape
    W = HEADS * HEAD_DIM
    per = tq // WINDOW_KEYS
    cur = pl.BlockSpec((None, tq, W), lambda b, r, i: (b, i, r))
    prv = pl.BlockSpec((None, WINDOW_KEYS, W), lambda b, r, i: (b, jnp.maximum(i * per - 1, 0), r))
    o, lse = pl.pallas_call(
        functools.partial(_band_body, tq=tq),
        out_shape=[jax.ShapeDtypeStruct((B, n, dil * W), BF16),
                   jax.ShapeDtypeStruct((B, dil, n, HEADS), F32)],
        grid=(B, dil, n // tq),
        in_specs=[cur, cur, cur, prv, prv],
        out_specs=[cur, pl.BlockSpec((None, None, tq, HEADS), lambda b, r, i: (b, r, i, 0))],
        compiler_params=_params(("parallel", "parallel", "arbitrary")), name="band_attention",
    )(q, k, v, k, v)
    return o, lse.transpose(0, 2, 1, 3).reshape(B, n * dil, HEADS)


def _cross_body(q_ref, mk_ref, mv_ref, o_ref, *, nb):
    slots = mk_ref.shape[1] // HEADS
    pairs = [(b, h) for b in range(nb) for h in range(HEADS)]
    head_rows = lambda h: pl.ds(h, slots, stride=HEADS)
    lanes = lambda h: slice(h * HEAD_DIM, (h + 1) * HEAD_DIM)
    scores = [_mm_nt(q_ref[b, :, lanes(h)], mk_ref[b, head_rows(h), :].astype(BF16))
              for b, h in pairs]
    probs = []
    for s in scores:
        p = jnp.exp(s - jnp.max(s, axis=-1, keepdims=True))
        probs.append((p.astype(BF16), jnp.sum(p, axis=-1, keepdims=True)))
    for (b, h), (p, den) in zip(pairs, probs):
        pv = _mm(p, mv_ref[b, head_rows(h), :].astype(BF16))
        o_ref[b, :, lanes(h)] = (pv / den).astype(o_ref.dtype)


def _cross_attention(q, mk, mv, *, tq, nb):
    B, L, W = q.shape
    tok = pl.BlockSpec((nb, tq, W), lambda b, i: (b, i, 0))
    mem = pl.BlockSpec((nb,) + mk.shape[1:], lambda b, i: (b, 0, 0))
    return pl.pallas_call(
        functools.partial(_cross_body, nb=nb), out_shape=jax.ShapeDtypeStruct((B, L, W), BF16),
        grid=(B // nb, L // tq), in_specs=[tok, mem, mem], out_specs=tok,
        compiler_params=_params(("parallel", "parallel")), name="cross_attention",
    )(q, mk, mv)


DECODE_ROW_GROUP = 16


DECODE_ROWS = 16


def _decode_body(q_ref, kn_ref, vn_ref, ck_ref, cv_ref, o_ref, lse_ref, *, dil, nnew, nb):
    grouped = len(ck_ref.shape) == 4
    nk = (nnew if grouped else dil) * WINDOW_KEYS
    qrow = lax.broadcasted_iota(jnp.int32, (DECODE_ROWS, nk), 0)
    kcol = lax.broadcasted_iota(jnp.int32, (DECODE_ROWS, nk), 1)
    if dil == 1:
        visible = kcol >= qrow
    elif grouped:
        visible = (kcol // WINDOW_KEYS) == qrow
    else:
        visible = (kcol % dil) == qrow
    new_q = lax.broadcasted_iota(jnp.int32, (nnew, 1), 0)
    head_new = lambda h: pl.ds(h, nnew, stride=HEADS)

    def cached(ref, b, h):
        if grouped:
            return jnp.concatenate([ref[b, :, c * HEADS + h, :] for c in range(nnew)], axis=0)
        return ref[b, pl.ds(h, nk, stride=HEADS), :]

    pairs = [(b, h) for b in range(nb) for h in range(HEADS)]
    pad = jnp.zeros((DECODE_ROWS - nnew, LANES), F32)
    queries = {bh: q_ref[bh[0], head_new(bh[1]), :] for bh in pairs}
    scores = {bh: _mm_nt(jnp.concatenate([queries[bh], pad], axis=0).astype(BF16),
                         cached(ck_ref, *bh).astype(BF16)) for bh in pairs}
    soft = {}
    for b, h in pairs:
        s = jnp.where(visible, scores[b, h], NEG)[:nnew]
        kn = kn_ref[b, head_new(h), :]
        sn = []
        for c in range(nnew):
            ok = (new_q >= c) if dil == 1 else (new_q == c)
            sn.append(jnp.where(ok, jnp.sum(queries[b, h] * kn[c:c + 1, :], axis=-1, keepdims=True),
                                NEG))
        m = jnp.max(s, axis=-1, keepdims=True)
        for t in sn:
            m = jnp.maximum(m, t)
        p = jnp.exp(s - m)
        pn = [jnp.exp(t - m) for t in sn]
        den = jnp.sum(p, axis=-1, keepdims=True)
        for t in pn:
            den = den + t
        p_rows = jnp.concatenate([p, jnp.zeros((DECODE_ROWS - nnew, nk), F32)], axis=0)
        soft[b, h] = (p_rows.astype(BF16), pn, den, m + jnp.log(den))
    for b, h in pairs:
        p, pn, den, lse = soft[b, h]
        acc = _mm(p, cached(cv_ref, b, h).astype(BF16))[:nnew]
        vn = vn_ref[b, head_new(h), :]
        for c in range(nnew):
            acc = acc + pn[c] * vn[c:c + 1, :]
        o_ref[b, head_new(h), :] = acc / den
        lse_ref[b, head_new(h), :] = jnp.broadcast_to(lse, (nnew, LANES))


def _decode_attention(q, kn, vn, ck, cv, window, dil, *, nb):
    B, nrow, _ = q.shape
    rows = window * HEADS
    small = pl.BlockSpec((nb, nrow, LANES), lambda b: (b, 0, 0))
    slot_rows = HEADS * dil
    if slot_rows > DECODE_ROW_GROUP:
        view = lambda c: c.reshape(B, WINDOW_KEYS, slot_rows, LANES)
        big = pl.BlockSpec((nb, WINDOW_KEYS, DECODE_ROW_GROUP, LANES), lambda b: (b, 0, 0, 0))
    else:
        view = lambda c: c
        big = pl.BlockSpec((nb, rows, LANES), lambda b: (b, 0, 0))
    return pl.pallas_call(
        functools.partial(_decode_body, dil=dil, nnew=nrow // HEADS, nb=nb),
        out_shape=[jax.ShapeDtypeStruct((B, nrow, LANES), F32),
                   jax.ShapeDtypeStruct((B, nrow, LANES), F32)],
        grid=(B // nb,), in_specs=[small, small, small, big, big],
        out_specs=[small, small],
        compiler_params=_params(("parallel",)), name="decode_attention",
    )(q, kn, vn, view(ck), view(cv))


def _roll_copies(step, nsteps, old, new, out, buf, sems):
    total, rows, _ = old.shape
    shift = new.shape[1]
    per = total // nsteps
    batches = pl.ds(step * per, per)
    slot = step % 2
    stage = buf.at[slot]
    reads = [
        pltpu.make_async_copy(old.at[batches, pl.ds(shift, rows - shift), :],
                              stage.at[:, pl.ds(0, rows - shift), :], sems.at[slot, 0]),
        pltpu.make_async_copy(new.at[batches], stage.at[:, pl.ds(rows - shift, shift), :],
                              sems.at[slot, 1]),
    ]
    return reads, pltpu.make_async_copy(stage, out.at[batches], sems.at[slot, 2])


def _with_rolls(body, nfixed_in, nfixed_out, nrolls, nsteps, step_fn):
    if nrolls == 0:
        return body

    def wrapped(*refs):
        ins = refs[:nfixed_in]
        pairs = refs[nfixed_in:nfixed_in + 2 * nrolls]
        base = nfixed_in + 2 * nrolls
        outs = refs[base:base + nfixed_out]
        rolled = refs[base + nfixed_out:base + nfixed_out + nrolls]
        staging = refs[base + nfixed_out + nrolls:base + nfixed_out + 3 * nrolls]
        rest = refs[base + nfixed_out + 3 * nrolls:]
        step = step_fn()
        args = [(pairs[2 * t], pairs[2 * t + 1], rolled[t], staging[2 * t + 1], staging[2 * t])
                for t in range(nrolls)]

        @pl.when(step >= 2)
        def _():
            for a in args:
                _roll_copies(step - 2, nsteps, *a)[1].wait()

        current = [_roll_copies(step, nsteps, *a) for a in args]
        for reads, _ in current:
            for c in reads:
                c.start()
        body(*ins, *outs, *rest)
        for reads, write in current:
            for c in reads:
                c.wait()
            write.start()

        @pl.when(step == nsteps - 1)
        def _():
            for a in args:
                _roll_copies(step - 1, nsteps, *a)[1].wait()
            for _, write in current:
                write.wait()

    return wrapped


def _roll_specs(rolls, nsteps):
    any_spec = pl.BlockSpec(memory_space=pl.ANY)
    extra, shapes, scratch = [], [], []
    for old, new in rolls:
        assert old.shape[0] % nsteps == 0 and nsteps >= 2
        extra += [old, new]
        shapes.append(jax.ShapeDtypeStruct(old.shape, old.dtype))
        scratch += [pltpu.SemaphoreType.DMA((2, 3)),
                    pltpu.VMEM((2, old.shape[0] // nsteps) + old.shape[1:], old.dtype)]
    return extra, [any_spec] * len(extra), shapes, [any_spec] * len(shapes), scratch


def _merge_body(x_ref, a_ref, o1_ref, o2_ref, o3_ref, l1_ref, l2_ref, l3_ref, ox_ref, g_ref,
                wa_ref, wb_ref, wc_ref, wo_ref, h_ref, *scratch, dils):
    tm = x_ref.shape[0]
    hw = HEADS * HEAD_DIM
    groups, si = [], 0
    for o_ref, d in zip((o1_ref, o2_ref, o3_ref), dils):
        if d == 1:
            groups.append([o_ref[:, h * HEAD_DIM:(h + 1) * HEAD_DIM] for h in range(HEADS)])
            continue
        buf = scratch[si]
        si += 1
        for r in range(d):
            for h in range(HEADS):
                lanes = slice(r * hw + h * HEAD_DIM, r * hw + (h + 1) * HEAD_DIM)
                buf[h, pl.ds(r, tm // d, stride=d), :] = o_ref[:, lanes].astype(F32)
        groups.append([buf[h] for h in range(HEADS)])
    l1, l2, l3 = l1_ref[...], l2_ref[...], l3_ref[...]
    lmax = jnp.maximum(jnp.maximum(l1, l2), l3)
    e1, e2, e3 = jnp.exp(l1 - lmax), jnp.exp(l2 - lmax), jnp.exp(l3 - lmax)
    inv = 1.0 / (e1 + e2 + e3)
    heads = []
    for h in range(HEADS):
        heads.append((e1[:, h:h + 1] * inv[:, h:h + 1]) * groups[0][h].astype(F32)
                     + (e2[:, h:h + 1] * inv[:, h:h + 1]) * groups[1][h].astype(F32)
                     + (e3[:, h:h + 1] * inv[:, h:h + 1]) * groups[2][h].astype(F32))
    o_dil = jnp.concatenate(heads, axis=1).astype(BF16)
    ya = _mm(a_ref[...], wa_ref[...])
    yb = _mm(o_dil, wb_ref[...])
    yc = _mm(ox_ref[...], wc_ref[...])
    d = D_MODEL
    mix = (g_ref[:, 0:d].astype(F32) * ya + g_ref[:, d:2 * d].astype(F32) * yb
           + g_ref[:, 2 * d:3 * d].astype(F32) * yc)
    h_ref[...] = x_ref[...] + _mm(mix.astype(BF16), wo_ref[...])


def _merge(x, a, o_groups, lse_groups, ox, gates, wa, wb, wc, wo, *, tm, dils, rolls=()):
    T = x.shape[0]
    nsteps = T // tm
    hw = HEADS * HEAD_DIM
    tok = lambda w: pl.BlockSpec((tm, w), lambda i: (i, 0))
    grp = lambda d: pl.BlockSpec((tm // d, d * hw), lambda i: (i, 0))
    full = lambda w: pl.BlockSpec(w.shape, lambda i: (0, 0))
    extra, extra_specs, roll_shapes, roll_specs, roll_scratch = _roll_specs(rolls, nsteps)
    body = _with_rolls(functools.partial(_merge_body, dils=dils), 14, 1, len(rolls), nsteps,
                       lambda: pl.program_id(0))
    out = pl.pallas_call(
        body, out_shape=[jax.ShapeDtypeStruct((T, D_MODEL), F32)] + roll_shapes, grid=(nsteps,),
        in_specs=[tok(D_MODEL), tok(HEADS * GLA_DV), grp(dils[0]), grp(dils[1]), grp(dils[2]),
                  tok(HEADS), tok(HEADS), tok(HEADS), tok(hw), tok(3 * D_MODEL),
                  full(wa), full(wb), full(wc), full(wo)] + extra_specs,
        out_specs=[tok(D_MODEL)] + roll_specs,
        scratch_shapes=roll_scratch + [pltpu.VMEM((HEADS, tm, HEAD_DIM), F32)
                                       for d in dils if d > 1],
        compiler_params=_params(("arbitrary",)), name="merge",
    )(x, a, *o_groups, *lse_groups, ox, gates, wa, wb, wc, wo, *extra)
    return out[0], out[1:]


def _ffn_body(h_ref, gn_ref, wu_ref, wd_ref, gf_ref, y_ref):
    h = h_ref[...]
    n = (h * gn_ref[...]).astype(BF16)
    inv_ms = 1.0 / (jnp.mean(h * h, axis=-1, keepdims=True) + EPS)
    acc = None
    for c0 in range(0, D_FF, COL_CHUNK):
        u = jnp.maximum(_mm(n, wu_ref[:, c0:c0 + COL_CHUNK]), 0.0)
        part = _mm((u * u).astype(BF16), wd_ref[c0:c0 + COL_CHUNK, :])
        acc = part if acc is None else acc + part
    y_ref[...] = _rms(h + inv_ms * acc, gf_ref[...])


def _ffn(h, gain_ffn, wu, wd, gain_final, *, tm, rolls=()):
    T = h.shape[0]
    nsteps = T // tm
    tok = pl.BlockSpec((tm, D_MODEL), lambda i: (i, 0))
    full = lambda w: pl.BlockSpec(w.shape, lambda i: (0, 0), pipeline_mode=pl.Buffered(1))
    extra, extra_specs, roll_shapes, roll_specs, roll_scratch = _roll_specs(rolls, nsteps)
    body = _with_rolls(_ffn_body, 5, 1, len(rolls), nsteps, lambda: pl.program_id(0))
    out = pl.pallas_call(
        body, out_shape=[jax.ShapeDtypeStruct((T, D_MODEL), F32)] + roll_shapes, grid=(nsteps,),
        in_specs=[tok, full(gain_ffn), full(wu), full(wd), full(gain_final)] + extra_specs,
        out_specs=[tok] + roll_specs, scratch_shapes=roll_scratch,
        compiler_params=_params(("arbitrary",)), name="ffn",
    )(h, gain_ffn, wu, wd, gain_final, *extra)
    return out[0], out[1:]


def _rope_tables(pos):
    half = HEAD_DIM // 2
    inv = ROPE_THETA ** (-jnp.arange(half, dtype=F32) / half)
    ang = pos.astype(F32)[:, None] * inv[None, :]
    cos, sin = jnp.cos(ang), jnp.sin(ang)
    return jnp.concatenate([cos, cos], axis=-1), jnp.concatenate([-sin, sin], axis=-1)


def _split_weights(w_in, w_decay):
    hw = HEADS * HEAD_DIM
    sizes = [hw, hw, HEADS * GLA_DV, HEADS * GLA_DV, GLA_RANK] + [hw] * 9 + [hw, 3 * D_MODEL]
    offs = [0]
    for s in sizes:
        offs.append(offs[-1] + s)
    cols = [w_in[:, offs[j]:offs[j + 1]].astype(BF16) for j in range(len(sizes))]
    cols[4] = jnp.pad(cols[4], ((0, 0), (0, LANES - GLA_RANK)))
    wdec = jnp.pad(w_decay.astype(BF16), ((0, LANES - GLA_RANK), (0, 0)))
    return cols, wdec


def _layer(x, pos, tail_rows, cols, wdec, b_decay, b_gate, norm_mix, tm, dils, rolls=()):
    cos, sin = _rope_tables(pos)
    gain = norm_mix[None, :]
    (gq, gk, gv, gr, lf), _ = _project(
        x, gain, cos, sin,
        [("lin", cols[0], QK_SCALE, BF16, 0, False, 1), ("lin", cols[1], 1.0, BF16, 0, False, 1),
         ("lin", cols[2], 1.0, BF16, 0, False, 1), ("lin", cols[3], 1.0, BF16, 0, False, 1),
         ("decay", cols[4], 1.0, F32, 0, False, 1)],
        tm=tm, wdec=wdec, bdec=b_decay[None, :])
    dil_segs = []
    for g in range(3):
        dil_segs += [("rope", cols[5 + 3 * g], QK_SCALE, BF16, 0, False, dils[g]),
                     ("rope", cols[6 + 3 * g], 1.0, BF16, 0, True, dils[g]),
                     ("lin", cols[7 + 3 * g], 1.0, BF16, 0, True, dils[g])]
    dil_out, _ = _project(x, gain, cos, sin, dil_segs, tm=tm, tail_rows=tail_rows)
    (xq, gates), rolled = _project(
        x, gain, cos, sin,
        [("lin", cols[14], QK_SCALE, BF16, 0, False, 1), ("sig", cols[15], 1.0, BF16, 0, False, 1)],
        tm=tm, bias=b_gate[None, :], rolls=rolls)
    dil = [dil_out[5 * g:5 * g + 5] for g in range(3)]
    return (gq, gk, gv, gr, lf), dil, xq, gates, rolled


def _finish(x2d, a, o_groups, lse_groups, ox, gates, post, tm, dils, merge_rolls=(),
            ffn_rolls=()):
    wa, wb, wc, wo, gain_ffn, wu, wd, gain_final = post
    flat = lambda t: t.reshape(-1, t.shape[-1])
    h, rolled_m = _merge(x2d, flat(a), [flat(t) for t in o_groups], [flat(t) for t in lse_groups],
                         flat(ox), flat(gates), wa, wb, wc, wo, tm=tm, dils=dils,
                         rolls=merge_rolls)
    y, rolled_f = _ffn(h, gain_ffn, wu, wd, gain_final, tm=tm, rolls=ffn_rolls)
    return y, rolled_m, rolled_f


def kernel(x_prompt, x_sample, mem_prompt, state_gla, cache_dil1_k, cache_dil1_v, cache_dil2_k, cache_dil2_v, cache_dil3_k, cache_dil3_v, cache_mem_k, cache_mem_v, norm_mix, w_in, b_gate, w_decay, b_decay, gla_head_norm, w_proj_gla, w_proj_dil, w_proj_x, norm_mem, w_mem_kv, w_out, norm_ffn, w_ffn_up, w_ffn_down, norm_final):
    B, L, D = x_prompt.shape
    SB, SL, _ = x_sample.shape
    depth = w_in.shape[0]
    assert depth == 1, "single trunk layer"
    i = 0
    cols, wdec = _split_weights(w_in[i], w_decay[i])
    post = (w_proj_gla[i].astype(BF16), w_proj_dil[i].astype(BF16), w_proj_x[i].astype(BF16),
            w_out[i].astype(BF16), norm_ffn[i][None, :], w_ffn_up[i].astype(BF16),
            w_ffn_down[i].astype(BF16), norm_final[None, :])
    head_norm = gla_head_norm[i][None, :]
    hw = HEADS * HEAD_DIM
    max_window = DIL_PATTERNS[-1][0]

    mem_rows = lambda t: t.reshape(t.shape[0], N_MEM * HEADS, HEAD_DIM)

    T = SB * SL
    pos_s = jnp.tile(PAST_LEN + jnp.arange(SL), SB)
    (gq, gk, gv, gr, lf), dil, xq, gates, _ = _layer(
        x_sample.reshape(1, T, D), pos_s, T, cols, wdec, b_decay[i], b_gate[i], norm_mix[i], T,
        (1, 1, 1))
    pad_rows = 16 - SL
    per_batch = lambda t: jnp.pad(t.reshape(SB, SL, t.shape[-1]), ((0, 0), (0, pad_rows), (0, 0)))
    a_s, state_s = _gla(per_batch(gq), per_batch(gk), per_batch(gv), per_batch(lf), per_batch(gr),
                        state_gla[i], head_norm, chunk=16, nb=8)
    a_s = a_s[:, :SL]
    caches = ((cache_dil1_k[i], cache_dil1_v[i]), (cache_dil2_k[i], cache_dil2_v[i]),
              (cache_dil3_k[i], cache_dil3_v[i]))
    o_groups, lse_groups, roll_pairs = [], [], []
    rows = lambda t: t.astype(F32).reshape(SB, SL * HEADS, HEAD_DIM)
    for (win, dl), (dq, dk, dk_tail, dv, dv_tail), (ck, cv), nb in zip(
            DIL_PATTERNS, dil, caches, (4, 2, 2)):
        ck, cv = (c.reshape(SB, win * HEADS, HEAD_DIM) for c in (ck, cv))
        o_g, lse_g = _decode_attention(rows(dq), rows(dk_tail), rows(dv_tail), ck, cv, win, dl,
                                       nb=nb)
        o_groups.append(o_g.reshape(SB, SL, hw).astype(BF16))
        lse_groups.append(lse_g[:, :, 0].reshape(SB, SL, HEADS))
        roll_pairs += [(ck, rows(dk_tail)), (cv, rows(dv_tail))]
    ox = _cross_attention(per_batch(xq), mem_rows(cache_mem_k[i]), mem_rows(cache_mem_v[i]),
                          tq=16, nb=8)[:, :SL]
    y_sample, _, _ = _finish(x_sample.reshape(T, D), a_s, o_groups, lse_groups, ox, gates, post,
                             T, (1, 1, 1))
    y_sample = y_sample.reshape(SB, SL, D)

    dils_p = tuple(dl for _, dl in DIL_PATTERNS)
    (gq, gk, gv, gr, lf), dil, xq, gates, rolled_v3 = _layer(
        x_prompt, jnp.arange(L), max_window, cols, wdec, b_decay[i], b_gate[i], norm_mix[i], 512,
        dils_p, rolls=roll_pairs[5:6])
    a_p, state_p = _gla(gq, gk, gv, lf, gr, jnp.zeros((B, HEADS, HEAD_DIM, GLA_DV), F32),
                        head_norm, chunk=128, nb=1)
    o_groups, lse_groups, bufs_p = [], [], []
    for (win, dl), (dq, dk, dk_tail, dv, dv_tail) in zip(DIL_PATTERNS, dil):
        o_g, lse_g = _band_attention(dq, dk, dv, dl, tq=512)
        o_groups.append(o_g)
        lse_groups.append(lse_g)
        keep = min(win, L)
        bufs_p += [dk_tail[:, max_window - keep:].reshape(1, B, keep, HEADS, HEAD_DIM),
                   dv_tail[:, max_window - keep:].reshape(1, B, keep, HEADS, HEAD_DIM)]
    zeros_tab = jnp.zeros((N_MEM, LANES), F32)
    w_mem = w_mem_kv[i].astype(BF16)
    (mk, mv), _ = _project(
        mem_prompt, norm_mem[i][None, :], zeros_tab, zeros_tab,
        [("lin", w_mem[:, :hw], 1.0, F32, 0, False, 1), ("lin", w_mem[:, hw:], 1.0, F32, 0, False, 1)],
        tm=N_MEM)
    ox = _cross_attention(xq, mem_rows(mk), mem_rows(mv), tq=512, nb=1)
    y_prompt, rolled_12, rolled_k3 = _finish(
        x_prompt.reshape(B * L, D), a_p, o_groups, lse_groups, ox, gates, post, 512, dils_p,
        merge_rolls=roll_pairs[:4], ffn_rolls=roll_pairs[4:5])
    y_prompt = y_prompt.reshape(B, L, D)
    bufs_s = [c.reshape(1, SB, c.shape[1] // HEADS, HEADS, HEAD_DIM)
              for c in (*rolled_12, *rolled_k3, *rolled_v3)]

    return (y_prompt, y_sample, state_p[None], *bufs_p,
            mk.reshape(1, B, N_MEM, HEADS, HEAD_DIM), mv.reshape(1, B, N_MEM, HEADS, HEAD_DIM),
            state_s[None], *bufs_s)
```

```python
import functools

import jax
import jax.numpy as jnp
from jax import lax
from jax.experimental import pallas as pl
from jax.experimental.pallas import tpu as pltpu

F32 = jnp.float32
BF16 = jnp.bfloat16

D_MODEL = 1024
PAST_LEN = 8192
N_MEM = 256
HEADS = 4
HEAD_DIM = 128
GLA_DV = 256
GLA_RANK = 16
GLA_GATE_NORM = 16.0
DIL_PATTERNS = ((128, 1), (512, 4), (2048, 16))
WINDOW_KEYS = 128
D_FF = 4 * D_MODEL
ROPE_THETA = 10000.0
EPS = 1e-6
NEG = -1e30
QK_SCALE = HEAD_DIM ** -0.5
LOG2_E = 1.4426950408889634

LANES = 128
COL_CHUNK = 512
VMEM_LIMIT = 48 * 1024 * 1024
VMEM_LIMIT_ROLL_HOST = 56 * 1024 * 1024


def _mm(a, b):
    return jnp.dot(a, b, preferred_element_type=F32)


def _mm_nt(a, b):
    return lax.dot_general(a, b, (((1,), (1,)), ((), ())), preferred_element_type=F32)


def _mm_tn(a, b):
    return lax.dot_general(a, b, (((0,), (0,)), ((), ())), preferred_element_type=F32)


def _rms(x, gain):
    return x * lax.rsqrt(jnp.mean(x * x, axis=-1, keepdims=True) + EPS) * gain


def _params(sem, vmem_limit=VMEM_LIMIT):
    return pltpu.CompilerParams(dimension_semantics=sem, vmem_limit_bytes=vmem_limit)


def _lane_blocks(y):
    return [y[:, j * LANES:(j + 1) * LANES] for j in range(y.shape[1] // LANES)]


def _residue_major(ref, dil):
    nblk, tm, _ = ref.shape
    per = tm // dil
    return jnp.concatenate(
        [jnp.concatenate([ref[j, pl.ds(r, per, stride=dil), :] for j in range(nblk)], axis=1)
         for r in range(dil)], axis=0)


def _store_token_order(buf, y, dil):
    per = y.shape[0] // dil
    for r in range(dil):
        for j, blk in enumerate(_lane_blocks(y[r * per:(r + 1) * per])):
            buf[j, pl.ds(r, per, stride=dil), :] = blk


def _proj_body(*refs, plan, first_tail_tile, dils):
    x_ref, g_ref, cos_ref, sin_ref, b_ref, wdec_ref, bdec_ref = refs[:7]
    w_refs = refs[7:7 + len(plan)]
    nout = sum(2 if p[4] else 1 for p in plan)
    out_refs = list(refs[7 + len(plan):7 + len(plan) + nout])
    scratch = refs[7 + len(plan) + nout:]
    tm, dm = x_ref.shape
    x = x_ref[...]
    xg = x * g_ref[...]
    inv_rms = jnp.broadcast_to(lax.rsqrt(jnp.mean(x * x, axis=-1, keepdims=True) + EPS),
                               (tm, LANES))
    if scratch:
        n_scr, tail_scr = scratch
        for j, blk in enumerate(_lane_blocks(xg) + [inv_rms]):
            n_scr[j] = blk
    rows_of = {1: xg.astype(BF16)}
    factor = {1: inv_rms}
    tables = {1: (cos_ref[...] * inv_rms, sin_ref[...] * inv_rms)}
    for d in dils:
        if d > 1:
            permuted = _residue_major(n_scr, d)
            rows_of[d] = permuted[:, :dm].astype(BF16)
            factor[d] = permuted[:, dm:]
            tables[d] = tuple(
                jnp.concatenate([t[pl.ds(r, tm // d, stride=d), :] for r in range(d)], axis=0)
                * factor[d] for t in (cos_ref, sin_ref))
    in_tail = pl.program_id(1) >= first_tail_tile
    oi = 0
    for (kind, scale, width, boff, has_tail, dil), w_ref in zip(plan, w_refs):
        out = out_refs[oi]
        oi += 1
        tail = None
        if has_tail:
            tail = out_refs[oi]
            oi += 1
        n = rows_of[dil]
        if kind == "decay":
            ga = (_mm(n, w_ref[...]) * factor[dil]).astype(BF16)
            z = _mm(ga, wdec_ref[...]) + bdec_ref[...]
            logsig = jnp.minimum(z, 0.0) - jnp.log1p(jnp.exp(-jnp.abs(z)))
            out[...] = logsig * (1.0 / GLA_GATE_NORM)
            continue
        per = tm // dil
        for c0 in range(0, width, COL_CHUNK):
            cw = min(COL_CHUNK, width - c0)
            y = _mm(n, w_ref[:, c0:c0 + cw])
            if kind == "rope":
                cosv, sinv = tables[dil]
                heads = []
                for j in range(cw // HEAD_DIM):
                    yh = y[:, j * HEAD_DIM:(j + 1) * HEAD_DIM]
                    heads.append(yh * cosv + pltpu.roll(yh, HEAD_DIM // 2, 1) * sinv)
                y = jnp.concatenate(heads, axis=1)
            else:
                y = y * jnp.concatenate([factor[dil]] * (cw // LANES), axis=1)
            if kind == "sig":
                y = jax.nn.sigmoid(y + b_ref[:, boff + c0:boff + c0 + cw])
            if tail is not None:
                @pl.when(in_tail)
                def _():
                    if dil == 1:
                        tail[:, c0:c0 + cw] = y
                    else:
                        _store_token_order(tail_scr, y, dil)
                        for j in range(cw // LANES):
                            tail[:, c0 + j * LANES:c0 + (j + 1) * LANES] = tail_scr[j]
            if scale != 1.0:
                y = y * scale
            for r in range(dil):
                out[:, r * width + c0:r * width + c0 + cw] = y[r * per:(r + 1) * per].astype(out.dtype)


def _project(x, gain, cos, sin, segs, *, tm, tail_rows=0, bias=None, wdec=None, bdec=None,
             rolls=()):
    B, L, D = x.shape
    nL = L // tm
    ntail = tail_rows // tm
    first_tail_tile = nL - ntail
    if bias is None:
        bias = jnp.zeros((1, LANES), F32)
    if wdec is None:
        wdec = jnp.zeros((LANES, LANES), BF16)
        bdec = jnp.zeros((1, LANES), F32)

    def const2(shape):
        return pl.BlockSpec(shape, lambda b, i: (0, 0), pipeline_mode=pl.Buffered(1))

    in_specs = [
        pl.BlockSpec((None, tm, D), lambda b, i: (b, i, 0)),
        const2((1, D)),
        pl.BlockSpec((tm, LANES), lambda b, i: (i, 0)),
        pl.BlockSpec((tm, LANES), lambda b, i: (i, 0)),
        const2(bias.shape), const2(wdec.shape), const2(bdec.shape),
    ]
    plan, weights, out_shapes, out_specs = [], [], [], []
    for kind, w, scale, dtype, boff, has_tail, dil in segs:
        width = w.shape[1]
        owidth = wdec.shape[1] if kind == "decay" else width
        plan.append((kind, scale, width, boff, has_tail, dil))
        weights.append(w)
        in_specs.append(const2(w.shape))
        out_shapes.append(jax.ShapeDtypeStruct((B, L // dil, dil * owidth), dtype))
        out_specs.append(pl.BlockSpec((None, tm // dil, dil * owidth), lambda b, i: (b, i, 0)))
        if has_tail:
            out_shapes.append(jax.ShapeDtypeStruct((B, tail_rows, width), F32))
            out_specs.append(pl.BlockSpec(
                (None, tm, width),
                lambda b, i: (b, jnp.maximum(i - first_tail_tile, 0), 0),
                pipeline_mode=pl.Buffered(1)))
    dils = tuple(sorted({p[5] for p in plan}))
    scratch = []
    if dils != (1,):
        scratch = [pltpu.VMEM((D // LANES + 1, tm, LANES), F32),
                   pltpu.VMEM((COL_CHUNK // LANES, tm, LANES), F32)]
    nsteps = B * nL
    extra, extra_specs, roll_shapes, roll_specs, roll_scratch = _roll_specs(rolls, nsteps)
    body = functools.partial(_proj_body, plan=tuple(plan), first_tail_tile=first_tail_tile,
                             dils=dils)
    body = _with_rolls(body, len(in_specs), len(out_shapes), len(rolls), nsteps,
                       lambda: pl.program_id(0) * nL + pl.program_id(1))
    out = pl.pallas_call(
        body, out_shape=out_shapes + roll_shapes, grid=(B, nL),
        in_specs=in_specs + extra_specs, out_specs=out_specs + roll_specs,
        scratch_shapes=roll_scratch + scratch,
        compiler_params=_params(("arbitrary", "arbitrary"),
                                VMEM_LIMIT_ROLL_HOST if rolls else VMEM_LIMIT), name="project",
    )(x, gain, cos, sin, bias, wdec, bdec, *weights, *extra)
    return out[:len(out_shapes)], out[len(out_shapes):]


GLA_SUB = 8


def _gla_level_masks(T):
    row = lax.broadcasted_iota(jnp.int32, (T, T), 0)
    col = lax.broadcasted_iota(jnp.int32, (T, T), 1)
    rid = lax.broadcasted_iota(jnp.int32, (T, 1), 0)
    levels = []
    bs = 2 * GLA_SUB
    while bs <= T:
        half = bs // 2
        second = (rid & (bs - 1)) >= half
        pair = ((row & -bs) == (col & -bs)) & ((row & (bs - 1)) >= half) & ((col & (bs - 1)) < half)
        levels.append((second, pair, bs))
        bs *= 2
    return levels


def _gla_matmuls(qf, kf, v, cum, s0, levels):
    T = qf.shape[0]
    last = cum[T - 1:T, :]
    o_inter = _mm((qf * jnp.exp2(cum)).astype(BF16), s0.astype(BF16))
    off = jnp.zeros((T, T), F32)
    for second, pair_mask, bs in levels:
        bound = []
        for j in range(T // bs):
            mid_row = j * bs + bs // 2 - 1
            mid = jnp.broadcast_to(cum[mid_row:mid_row + 1, :], (GLA_SUB, HEAD_DIM))
            bound += [mid] * (bs // GLA_SUB)
        rel = cum - jnp.concatenate(bound, axis=0)
        e = jnp.exp2(jnp.where(second, rel, -rel))
        off = jnp.where(pair_mask, _mm_nt((qf * e).astype(BF16), (kf * e).astype(BF16)), off)
    upd = _mm_tn((kf * jnp.exp2(last - cum)).astype(BF16), v)
    er = lax.broadcasted_iota(jnp.int32, (HEAD_DIM, HEAD_DIM), 0)
    ec = lax.broadcasted_iota(jnp.int32, (HEAD_DIM, HEAD_DIM), 1)
    decay_col = jnp.sum(jnp.where(er == ec, jnp.exp2(last), 0.0), axis=1, keepdims=True)
    return o_inter, off, decay_col * s0 + upd


def _gla_diagonal(qf, kf, cum, off, diag_masks):
    T = qf.shape[0]
    tiles = (T // GLA_SUB, GLA_SUB, HEAD_DIM)
    q3, k3, c3 = qf.reshape(tiles), kf.reshape(tiles), cum.reshape(tiles)
    att3 = off.reshape(T // GLA_SUB, GLA_SUB, T)
    for j in range(GLA_SUB):
        if j == 0:
            prod = q3 * k3
        else:
            prod = q3 * pltpu.roll(k3, j, 1) * jnp.exp2(c3 - pltpu.roll(c3, j, 1))
        att3 = jnp.where(diag_masks[j], jnp.sum(prod, axis=-1, keepdims=True), att3)
    return att3.reshape(T, T)


def _gla_body(q_ref, k_ref, v_ref, lf_ref, r_ref, s0_ref, hn_ref, a_ref, s_ref, *, nb):
    T = q_ref.shape[1]
    hw = HEADS * HEAD_DIM
    first = pl.program_id(1) == 0
    row = lax.broadcasted_iota(jnp.int32, (T, T), 0)
    col = lax.broadcasted_iota(jnp.int32, (T, T), 1)
    tri = jnp.where(row >= col, 1.0, 0.0).astype(BF16)
    levels = _gla_level_masks(T)
    shape3 = (T // GLA_SUB, GLA_SUB, T)
    sub_row = lax.broadcasted_iota(jnp.int32, shape3, 1)
    token = lax.broadcasted_iota(jnp.int32, shape3, 0) * GLA_SUB + sub_row
    key = lax.broadcasted_iota(jnp.int32, shape3, 2)
    diag_masks = [(key == token - j) & (sub_row >= j) for j in range(GLA_SUB)]

    @pl.when(first)
    def _():
        s_ref[...] = s0_ref[...]

    cums = []
    for b in range(nb):
        lf = lf_ref[b] * LOG2_E
        hi = lf.astype(BF16)
        lo = (lf - hi.astype(F32)).astype(BF16)
        parts = _mm(tri, jnp.concatenate([hi, lo], axis=1))
        cums.append(parts[:, :hw] + parts[:, hw:])
    pairs = [(b, h) for b in range(nb) for h in range(HEADS)]
    dk = lambda h: slice(h * HEAD_DIM, (h + 1) * HEAD_DIM)
    dv = lambda h: slice(h * GLA_DV, (h + 1) * GLA_DV)
    qk = {(b, h): (q_ref[b, :, dk(h)].astype(F32), k_ref[b, :, dk(h)].astype(F32), cums[b][:, dk(h)])
          for b, h in pairs}
    stage1 = {}
    for b, h in pairs:
        qf, kf, cum = qk[b, h]
        stage1[b, h] = _gla_matmuls(qf, kf, v_ref[b, :, dv(h)], cum, s_ref[b, h], levels)
        s_ref[b, h] = stage1[b, h][2]
    att = {}
    for b, h in pairs:
        qf, kf, cum = qk[b, h]
        att[b, h] = _gla_diagonal(qf, kf, cum, stage1[b, h][1], diag_masks).astype(BF16)
    for b, h in pairs:
        o = stage1[b, h][0] + _mm(att[b, h], v_ref[b, :, dv(h)])
        r = r_ref[b, :, dv(h)].astype(F32)
        a_ref[b, :, dv(h)] = (_rms(o, hn_ref[...]) * (r * jax.nn.sigmoid(r))).astype(a_ref.dtype)


def _gla(q, k, v, lf, r, s0, head_norm, *, chunk, nb):
    B, L, hw = q.shape
    tok = lambda w: pl.BlockSpec((nb, chunk, w), lambda b, c: (b, c, 0))
    state = pl.BlockSpec((nb, HEADS, HEAD_DIM, GLA_DV), lambda b, c: (b, 0, 0, 0))
    return pl.pallas_call(
        functools.partial(_gla_body, nb=nb),
        out_shape=[jax.ShapeDtypeStruct((B, L, HEADS * GLA_DV), BF16),
                   jax.ShapeDtypeStruct(s0.shape, F32)],
        grid=(B // nb, L // chunk),
        in_specs=[tok(512), tok(512), tok(1024), tok(512), tok(1024), state,
                  pl.BlockSpec((1, GLA_DV), lambda b, c: (0, 0))],
        out_specs=[tok(1024), state],
        compiler_params=_params(("parallel", "arbitrary")), name="gla",
    )(q, k, v, lf, r, s0, head_norm)


def _band_body(q_ref, k_ref, v_ref, kp_ref, vp_ref, o_ref, lse_ref, *, tq):
    step = pl.program_id(2)
    nblk = tq // WINDOW_KEYS
    t = lax.broadcasted_iota(jnp.int32, (WINDOW_KEYS, 2 * WINDOW_KEYS), 0)
    c = lax.broadcasted_iota(jnp.int32, (WINDOW_KEYS, 2 * WINDOW_KEYS), 1)
    band = (c >= t) & (c <= t + WINDOW_KEYS)
    lane = lax.broadcasted_iota(jnp.int32, (WINDOW_KEYS, LANES), 1)
    rows = lambda j: slice(j * WINDOW_KEYS, (j + 1) * WINDOW_KEYS)
    hd = lambda h: slice(h * HEAD_DIM, (h + 1) * HEAD_DIM)

    def window(cur_ref, prev_ref, j, h):
        before = prev_ref[:, hd(h)] if j == 0 else cur_ref[rows(j - 1), hd(h)]
        return jnp.concatenate([before, cur_ref[rows(j), hd(h)]], axis=0)

    pairs = [(j, h) for j in range(nblk) for h in range(HEADS)]
    scores = [_mm_nt(q_ref[rows(j), hd(h)], window(k_ref, kp_ref, j, h)) for j, h in pairs]
    probs = []
    for (j, h), s in zip(pairs, scores):
        first_key = step * tq + (j - 1) * WINDOW_KEYS
        s = jnp.where(band & (c + first_key >= 0), s, NEG)
        m = jnp.max(s, axis=-1, keepdims=True)
        p = jnp.exp(s - m)
        den = jnp.sum(p, axis=-1, keepdims=True)
        probs.append((p.astype(BF16), den, m + jnp.log(den)))
    lse_blk = [jnp.zeros((WINDOW_KEYS, LANES), F32) for _ in range(nblk)]
    for (j, h), (p, den, lse) in zip(pairs, probs):
        o_ref[rows(j), hd(h)] = (_mm(p, window(v_ref, vp_ref, j, h)) / den).astype(o_ref.dtype)
        lse_blk[j] = jnp.where(lane == h, lse, lse_blk[j])
    for j in range(nblk):
        lse_ref[rows(j), :] = lse_blk[j][:, :HEADS]


def _band_attention(q, k, v, dil, *, tq):
    B, n, _ = q.shape
    W = HEADS * HEAD_DIM
    per = tq // WINDOW_KEYS
    cur = pl.BlockSpec((None, tq, W), lambda b, r, i: (b, i, r))
    prv = pl.BlockSpec((None, WINDOW_KEYS, W), lambda b, r, i: (b, jnp.maximum(i * per - 1, 0), r))
    o, lse = pl.pallas_call(
        functools.partial(_band_body, tq=tq),
        out_shape=[jax.ShapeDtypeStruct((B, n, dil * W), BF16),
                   jax.ShapeDtypeStruct((B, dil, n, HEADS), F32)],
        grid=(B, dil, n // tq),
        in_specs=[cur, cur, cur, prv, prv],
        out_specs=[cur, pl.BlockSpec((None, None, tq, HEADS), lambda b, r, i: (b, r, i, 0))],
        compiler_params=_params(("parallel", "parallel", "arbitrary")), name="band_attention",
    )(q, k, v, k, v)
    return o, lse.transpose(0, 2, 1, 3).reshape(B, n * dil, HEADS)


def _cross_body(q_ref, mk_ref, mv_ref, o_ref, *, nb):
    slots = mk_ref.shape[1] // HEADS
    pairs = [(b, h) for b in range(nb) for h in range(HEADS)]
    head_rows = lambda h: pl.ds(h, slots, stride=HEADS)
    lanes = lambda h: slice(h * HEAD_DIM, (h + 1) * HEAD_DIM)
    scores = [_mm_nt(q_ref[b, :, lanes(h)], mk_ref[b, head_rows(h), :].astype(BF16))
              for b, h in pairs]
    probs = []
    for s in scores:
        p = jnp.exp(s - jnp.max(s, axis=-1, keepdims=True))
        probs.append((p.astype(BF16), jnp.sum(p, axis=-1, keepdims=True)))
    for (b, h), (p, den) in zip(pairs, probs):
        pv = _mm(p, mv_ref[b, head_rows(h), :].astype(BF16))
        o_ref[b, :, lanes(h)] = (pv / den).astype(o_ref.dtype)


def _cross_attention(q, mk, mv, *, tq, nb):
    B, L, W = q.shape
    tok = pl.BlockSpec((nb, tq, W), lambda b, i: (b, i, 0))
    mem = pl.BlockSpec((nb,) + mk.shape[1:], lambda b, i: (b, 0, 0))
    return pl.pallas_call(
        functools.partial(_cross_body, nb=nb), out_shape=jax.ShapeDtypeStruct((B, L, W), BF16),
        grid=(B // nb, L // tq), in_specs=[tok, mem, mem], out_specs=tok,
        compiler_params=_params(("parallel", "parallel")), name="cross_attention",
    )(q, mk, mv)


DECODE_ROW_GROUP = 16


DECODE_ROWS = 16


def _decode_body(q_ref, kn_ref, vn_ref, ck_ref, cv_ref, o_ref, lse_ref, *, dil, nnew, nb):
    grouped = len(ck_ref.shape) == 4
    nk = (nnew if grouped else dil) * WINDOW_KEYS
    qrow = lax.broadcasted_iota(jnp.int32, (DECODE_ROWS, nk), 0)
    kcol = lax.broadcasted_iota(jnp.int32, (DECODE_ROWS, nk), 1)
    if dil == 1:
        visible = kcol >= qrow
    elif grouped:
        visible = (kcol // WINDOW_KEYS) == qrow
    else:
        visible = (kcol % dil) == qrow
    new_q = lax.broadcasted_iota(jnp.int32, (nnew, 1), 0)
    head_new = lambda h: pl.ds(h, nnew, stride=HEADS)

    def cached(ref, b, h):
        if grouped:
            return jnp.concatenate([ref[b, :, c * HEADS + h, :] for c in range(nnew)], axis=0)
        return ref[b, pl.ds(h, nk, stride=HEADS), :]

    pairs = [(b, h) for b in range(nb) for h in range(HEADS)]
    pad = jnp.zeros((DECODE_ROWS - nnew, LANES), F32)
    queries = {bh: q_ref[bh[0], head_new(bh[1]), :] for bh in pairs}
    scores = {bh: _mm_nt(jnp.concatenate([queries[bh], pad], axis=0).astype(BF16),
                         cached(ck_ref, *bh).astype(BF16)) for bh in pairs}
    soft = {}
    for b, h in pairs:
        s = jnp.where(visible, scores[b, h], NEG)[:nnew]
        kn = kn_ref[b, head_new(h), :]
        sn = []
        for c in range(nnew):
            ok = (new_q >= c) if dil == 1 else (new_q == c)
            sn.append(jnp.where(ok, jnp.sum(queries[b, h] * kn[c:c + 1, :], axis=-1, keepdims=True),
                                NEG))
        m = jnp.max(s, axis=-1, keepdims=True)
        for t in sn:
            m = jnp.maximum(m, t)
        p = jnp.exp(s - m)
        pn = [jnp.exp(t - m) for t in sn]
        den = jnp.sum(p, axis=-1, keepdims=True)
        for t in pn:
            den = den + t
        p_rows = jnp.concatenate([p, jnp.zeros((DECODE_ROWS - nnew, nk), F32)], axis=0)
        soft[b, h] = (p_rows.astype(BF16), pn, den, m + jnp.log(den))
    for b, h in pairs:
        p, pn, den, lse = soft[b, h]
        acc = _mm(p, cached(cv_ref, b, h).astype(BF16))[:nnew]
        vn = vn_ref[b, head_new(h), :]
        for c in range(nnew):
            acc = acc + pn[c] * vn[c:c + 1, :]
        o_ref[b, head_new(h), :] = acc / den
        lse_ref[b, head_new(h), :] = jnp.broadcast_to(lse, (nnew, LANES))


def _decode_attention(q, kn, vn, ck, cv, window, dil, *, nb):
    B, nrow, _ = q.shape
    rows = window * HEADS
    small = pl.BlockSpec((nb, nrow, LANES), lambda b: (b, 0, 0))
    slot_rows = HEADS * dil
    if slot_rows > DECODE_ROW_GROUP:
        view = lambda c: c.reshape(B, WINDOW_KEYS, slot_rows, LANES)
        big = pl.BlockSpec((nb, WINDOW_KEYS, DECODE_ROW_GROUP, LANES), lambda b: (b, 0, 0, 0))
    else:
        view = lambda c: c
        big = pl.BlockSpec((nb, rows, LANES), lambda b: (b, 0, 0))
    return pl.pallas_call(
        functools.partial(_decode_body, dil=dil, nnew=nrow // HEADS, nb=nb),
        out_shape=[jax.ShapeDtypeStruct((B, nrow, LANES), F32),
                   jax.ShapeDtypeStruct((B, nrow, LANES), F32)],
        grid=(B // nb,), in_specs=[small, small, small, big, big],
        out_specs=[small, small],
        compiler_params=_params(("parallel",)), name="decode_attention",
    )(q, kn, vn, view(ck), view(cv))


def _roll_copies(step, nsteps, old, new, out, buf, sems):
    total, rows, _ = old.shape
    shift = new.shape[1]
    per = total // nsteps
    batches = pl.ds(step * per, per)
    slot = step % 2
    stage = buf.at[slot]
    reads = [
        pltpu.make_async_copy(old.at[batches, pl.ds(shift, rows - shift), :],
                              stage.at[:, pl.ds(0, rows - shift), :], sems.at[slot, 0]),
        pltpu.make_async_copy(new.at[batches], stage.at[:, pl.ds(rows - shift, shift), :],
                              sems.at[slot, 1]),
    ]
    return reads, pltpu.make_async_copy(stage, out.at[batches], sems.at[slot, 2])


def _with_rolls(body, nfixed_in, nfixed_out, nrolls, nsteps, step_fn):
    if nrolls == 0:
        return body

    def wrapped(*refs):
        ins = refs[:nfixed_in]
        pairs = refs[nfixed_in:nfixed_in + 2 * nrolls]
        base = nfixed_in + 2 * nrolls
        outs = refs[base:base + nfixed_out]
        rolled = refs[base + nfixed_out:base + nfixed_out + nrolls]
        staging = refs[base + nfixed_out + nrolls:base + nfixed_out + 3 * nrolls]
        rest = refs[base + nfixed_out + 3 * nrolls:]
        step = step_fn()
        args = [(pairs[2 * t], pairs[2 * t + 1], rolled[t], staging[2 * t + 1], staging[2 * t])
                for t in range(nrolls)]

        @pl.when(step >= 2)
        def _():
            for a in args:
                _roll_copies(step - 2, nsteps, *a)[1].wait()

        current = [_roll_copies(step, nsteps, *a) for a in args]
        for reads, _ in current:
            for c in reads:
                c.start()
        body(*ins, *outs, *rest)
        for reads, write in current:
            for c in reads:
                c.wait()
            write.start()

        @pl.when(step == nsteps - 1)
        def _():
            for a in args:
                _roll_copies(step - 1, nsteps, *a)[1].wait()
            for _, write in current:
                write.wait()

    return wrapped


def _roll_specs(rolls, nsteps):
    any_spec = pl.BlockSpec(memory_space=pl.ANY)
    extra, shapes, scratch = [], [], []
    for old, new in rolls:
        assert old.shape[0] % nsteps == 0 and nsteps >= 2
        extra += [old, new]
        shapes.append(jax.ShapeDtypeStruct(old.shape, old.dtype))
        scratch += [pltpu.SemaphoreType.DMA((2, 3)),
                    pltpu.VMEM((2, old.shape[0] // nsteps) + old.shape[1:], old.dtype)]
    return extra, [any_spec] * len(extra), shapes, [any_spec] * len(shapes), scratch


def _merge_body(x_ref, a_ref, o1_ref, o2_ref, o3_ref, l1_ref, l2_ref, l3_ref, ox_ref, g_ref,
                wa_ref, wb_ref, wc_ref, wo_ref, h_ref, *scratch, dils):
    tm = x_ref.shape[0]
    hw = HEADS * HEAD_DIM
    groups, si = [], 0
    for o_ref, d in zip((o1_ref, o2_ref, o3_ref), dils):
        if d == 1:
            groups.append([o_ref[:, h * HEAD_DIM:(h + 1) * HEAD_DIM] for h in range(HEADS)])
            continue
        buf = scratch[si]
        si += 1
        for r in range(d):
            for h in range(HEADS):
                lanes = slice(r * hw + h * HEAD_DIM, r * hw + (h + 1) * HEAD_DIM)
                buf[h, pl.ds(r, tm // d, stride=d), :] = o_ref[:, lanes].astype(F32)
        groups.append([buf[h] for h in range(HEADS)])
    l1, l2, l3 = l1_ref[...], l2_ref[...], l3_ref[...]
    lmax = jnp.maximum(jnp.maximum(l1, l2), l3)
    e1, e2, e3 = jnp.exp(l1 - lmax), jnp.exp(l2 - lmax), jnp.exp(l3 - lmax)
    inv = 1.0 / (e1 + e2 + e3)
    heads = []
    for h in range(HEADS):
        heads.append((e1[:, h:h + 1] * inv[:, h:h + 1]) * groups[0][h].astype(F32)
                     + (e2[:, h:h + 1] * inv[:, h:h + 1]) * groups[1][h].astype(F32)
                     + (e3[:, h:h + 1] * inv[:, h:h + 1]) * groups[2][h].astype(F32))
    o_dil = jnp.concatenate(heads, axis=1).astype(BF16)
    ya = _mm(a_ref[...], wa_ref[...])
    yb = _mm(o_dil, wb_ref[...])
    yc = _mm(ox_ref[...], wc_ref[...])
    d = D_MODEL
    mix = (g_ref[:, 0:d].astype(F32) * ya + g_ref[:, d:2 * d].astype(F32) * yb
           + g_ref[:, 2 * d:3 * d].astype(F32) * yc)
    h_ref[...] = x_ref[...] + _mm(mix.astype(BF16), wo_ref[...])


def _merge(x, a, o_groups, lse_groups, ox, gates, wa, wb, wc, wo, *, tm, dils, rolls=()):
    T = x.shape[0]
    nsteps = T // tm
    hw = HEADS * HEAD_DIM
    tok = lambda w: pl.BlockSpec((tm, w), lambda i: (i, 0))
    grp = lambda d: pl.BlockSpec((tm // d, d * hw), lambda i: (i, 0))
    full = lambda w: pl.BlockSpec(w.shape, lambda i: (0, 0))
    extra, extra_specs, roll_shapes, roll_specs, roll_scratch = _roll_specs(rolls, nsteps)
    body = _with_rolls(functools.partial(_merge_body, dils=dils), 14, 1, len(rolls), nsteps,
                       lambda: pl.program_id(0))
    out = pl.pallas_call(
        body, out_shape=[jax.ShapeDtypeStruct((T, D_MODEL), F32)] + roll_shapes, grid=(nsteps,),
        in_specs=[tok(D_MODEL), tok(HEADS * GLA_DV), grp(dils[0]), grp(dils[1]), grp(dils[2]),
                  tok(HEADS), tok(HEADS), tok(HEADS), tok(hw), tok(3 * D_MODEL),
                  full(wa), full(wb), full(wc), full(wo)] + extra_specs,
        out_specs=[tok(D_MODEL)] + roll_specs,
        scratch_shapes=roll_scratch + [pltpu.VMEM((HEADS, tm, HEAD_DIM), F32)
                                       for d in dils if d > 1],
        compiler_params=_params(("arbitrary",)), name="merge",
    )(x, a, *o_groups, *lse_groups, ox, gates, wa, wb, wc, wo, *extra)
    return out[0], out[1:]


def _ffn_body(h_ref, gn_ref, wu_ref, wd_ref, gf_ref, y_ref):
    h = h_ref[...]
    n = (h * gn_ref[...]).astype(BF16)
    inv_ms = 1.0 / (jnp.mean(h * h, axis=-1, keepdims=True) + EPS)
    acc = None
    for c0 in range(0, D_FF, COL_CHUNK):
        u = jnp.maximum(_mm(n, wu_ref[:, c0:c0 + COL_CHUNK]), 0.0)
        part = _mm((u * u).astype(BF16), wd_ref[c0:c0 + COL_CHUNK, :])
        acc = part if acc is None else acc + part
    y_ref[...] = _rms(h + inv_ms * acc, gf_ref[...])


def _ffn(h, gain_ffn, wu, wd, gain_final, *, tm, rolls=()):
    T = h.shape[0]
    nsteps = T // tm
    tok = pl.BlockSpec((tm, D_MODEL), lambda i: (i, 0))
    full = lambda w: pl.BlockSpec(w.shape, lambda i: (0, 0), pipeline_mode=pl.Buffered(1))
    extra, extra_specs, roll_shapes, roll_specs, roll_scratch = _roll_specs(rolls, nsteps)
    body = _with_rolls(_ffn_body, 5, 1, len(rolls), nsteps, lambda: pl.program_id(0))
    out = pl.pallas_call(
        body, out_shape=[jax.ShapeDtypeStruct((T, D_MODEL), F32)] + roll_shapes, grid=(nsteps,),
        in_specs=[tok, full(gain_ffn), full(wu), full(wd), full(gain_final)] + extra_specs,
        out_specs=[tok] + roll_specs, scratch_shapes=roll_scratch,
        compiler_params=_params(("arbitrary",)), name="ffn",
    )(h, gain_ffn, wu, wd, gain_final, *extra)
    return out[0], out[1:]


def _rope_tables(pos):
    half = HEAD_DIM // 2
    inv = ROPE_THETA ** (-jnp.arange(half, dtype=F32) / half)
    ang = pos.astype(F32)[:, None] * inv[None, :]
    cos, sin = jnp.cos(ang), jnp.sin(ang)
    return jnp.concatenate([cos, cos], axis=-1), jnp.concatenate([-sin, sin], axis=-1)


def _split_weights(w_in, w_decay):
    hw = HEADS * HEAD_DIM
    sizes = [hw, hw, HEADS * GLA_DV, HEADS * GLA_DV, GLA_RANK] + [hw] * 9 + [hw, 3 * D_MODEL]
    offs = [0]
    for s in sizes:
        offs.append(offs[-1] + s)
    cols = [w_in[:, offs[j]:offs[j + 1]].astype(BF16) for j in range(len(sizes))]
    cols[4] = jnp.pad(cols[4], ((0, 0), (0, LANES - GLA_RANK)))
    wdec = jnp.pad(w_decay.astype(BF16), ((0, LANES - GLA_RANK), (0, 0)))
    return cols, wdec


def _layer(x, pos, tail_rows, cols, wdec, b_decay, b_gate, norm_mix, tm, dils,
           rolls=((), (), ())):
    cos, sin = _rope_tables(pos)
    gain = norm_mix[None, :]
    (gq, gk, gv, gr, lf), rolled_a = _project(
        x, gain, cos, sin,
        [("lin", cols[0], QK_SCALE, BF16, 0, False, 1), ("lin", cols[1], 1.0, BF16, 0, False, 1),
         ("lin", cols[2], 1.0, BF16, 0, False, 1), ("lin", cols[3], 1.0, BF16, 0, False, 1),
         ("decay", cols[4], 1.0, F32, 0, False, 1)],
        tm=tm, wdec=wdec, bdec=b_decay[None, :], rolls=rolls[0])
    dil_segs = []
    for g in range(3):
        dil_segs += [("rope", cols[5 + 3 * g], QK_SCALE, BF16, 0, False, dils[g]),
                     ("rope", cols[6 + 3 * g], 1.0, BF16, 0, True, dils[g]),
                     ("lin", cols[7 + 3 * g], 1.0, BF16, 0, True, dils[g])]
    dil_out, rolled_b = _project(x, gain, cos, sin, dil_segs, tm=tm, tail_rows=tail_rows,
                                 rolls=rolls[1])
    (xq, gates), rolled_c = _project(
        x, gain, cos, sin,
        [("lin", cols[14], QK_SCALE, BF16, 0, False, 1), ("sig", cols[15], 1.0, BF16, 0, False, 1)],
        tm=tm, bias=b_gate[None, :], rolls=rolls[2])
    dil = [dil_out[5 * g:5 * g + 5] for g in range(3)]
    return (gq, gk, gv, gr, lf), dil, xq, gates, (rolled_a, rolled_b, rolled_c)


def _finish(x2d, a, o_groups, lse_groups, ox, gates, post, tm, dils, merge_rolls=(),
            ffn_rolls=()):
    wa, wb, wc, wo, gain_ffn, wu, wd, gain_final = post
    flat = lambda t: t.reshape(-1, t.shape[-1])
    h, rolled_m = _merge(x2d, flat(a), [flat(t) for t in o_groups], [flat(t) for t in lse_groups],
                         flat(ox), flat(gates), wa, wb, wc, wo, tm=tm, dils=dils,
                         rolls=merge_rolls)
    y, rolled_f = _ffn(h, gain_ffn, wu, wd, gain_final, tm=tm, rolls=ffn_rolls)
    return y, rolled_m, rolled_f


def kernel(x_prompt, x_sample, mem_prompt, state_gla, cache_dil1_k, cache_dil1_v, cache_dil2_k, cache_dil2_v, cache_dil3_k, cache_dil3_v, cache_mem_k, cache_mem_v, norm_mix, w_in, b_gate, w_decay, b_decay, gla_head_norm, w_proj_gla, w_proj_dil, w_proj_x, norm_mem, w_mem_kv, w_out, norm_ffn, w_ffn_up, w_ffn_down, norm_final):
    B, L, D = x_prompt.shape
    SB, SL, _ = x_sample.shape
    depth = w_in.shape[0]
    assert depth == 1, "single trunk layer"
    i = 0
    cols, wdec = _split_weights(w_in[i], w_decay[i])
    post = (w_proj_gla[i].astype(BF16), w_proj_dil[i].astype(BF16), w_proj_x[i].astype(BF16),
            w_out[i].astype(BF16), norm_ffn[i][None, :], w_ffn_up[i].astype(BF16),
            w_ffn_down[i].astype(BF16), norm_final[None, :])
    head_norm = gla_head_norm[i][None, :]
    hw = HEADS * HEAD_DIM
    max_window = DIL_PATTERNS[-1][0]

    mem_rows = lambda t: t.reshape(t.shape[0], N_MEM * HEADS, HEAD_DIM)

    T = SB * SL
    pos_s = jnp.tile(PAST_LEN + jnp.arange(SL), SB)
    (gq, gk, gv, gr, lf), dil, xq, gates, _ = _layer(
        x_sample.reshape(1, T, D), pos_s, T, cols, wdec, b_decay[i], b_gate[i], norm_mix[i], T,
        (1, 1, 1))
    pad_rows = 16 - SL
    per_batch = lambda t: jnp.pad(t.reshape(SB, SL, t.shape[-1]), ((0, 0), (0, pad_rows), (0, 0)))
    a_s, state_s = _gla(per_batch(gq), per_batch(gk), per_batch(gv), per_batch(lf), per_batch(gr),
                        state_gla[i], head_norm, chunk=16, nb=8)
    a_s = a_s[:, :SL]
    caches = ((cache_dil1_k[i], cache_dil1_v[i]), (cache_dil2_k[i], cache_dil2_v[i]),
              (cache_dil3_k[i], cache_dil3_v[i]))
    o_groups, lse_groups, roll_pairs = [], [], []
    rows = lambda t: t.astype(F32).reshape(SB, SL * HEADS, HEAD_DIM)
    for (win, dl), (dq, dk, dk_tail, dv, dv_tail), (ck, cv), nb in zip(
            DIL_PATTERNS, dil, caches, (4, 2, 2)):
        ck, cv = (c.reshape(SB, win * HEADS, HEAD_DIM) for c in (ck, cv))
        o_g, lse_g = _decode_attention(rows(dq), rows(dk_tail), rows(dv_tail), ck, cv, win, dl,
                                       nb=nb)
        o_groups.append(o_g.reshape(SB, SL, hw).astype(BF16))
        lse_groups.append(lse_g[:, :, 0].reshape(SB, SL, HEADS))
        roll_pairs += [(ck, rows(dk_tail)), (cv, rows(dv_tail))]
    ox = _cross_attention(per_batch(xq), mem_rows(cache_mem_k[i]), mem_rows(cache_mem_v[i]),
                          tq=16, nb=8)[:, :SL]
    y_sample, _, _ = _finish(x_sample.reshape(T, D), a_s, o_groups, lse_groups, ox, gates, post,
                             T, (1, 1, 1))
    y_sample = y_sample.reshape(SB, SL, D)

    d1k, d1v, d2k, d2v, d3k, d3v = roll_pairs
    dils_p = tuple(dl for _, dl in DIL_PATTERNS)
    (gq, gk, gv, gr, lf), dil, xq, gates, ((r_d2v,), (r_d3v,), (r_d2k,)) = _layer(
        x_prompt, jnp.arange(L), max_window, cols, wdec, b_decay[i], b_gate[i], norm_mix[i], 512,
        dils_p, rolls=([d2v], [d3v], [d2k]))
    a_p, state_p = _gla(gq, gk, gv, lf, gr, jnp.zeros((B, HEADS, HEAD_DIM, GLA_DV), F32),
                        head_norm, chunk=128, nb=1)
    o_groups, lse_groups, bufs_p = [], [], []
    for (win, dl), (dq, dk, dk_tail, dv, dv_tail) in zip(DIL_PATTERNS, dil):
        o_g, lse_g = _band_attention(dq, dk, dv, dl, tq=512)
        o_groups.append(o_g)
        lse_groups.append(lse_g)
        keep = min(win, L)
        bufs_p += [dk_tail[:, max_window - keep:].reshape(1, B, keep, HEADS, HEAD_DIM),
                   dv_tail[:, max_window - keep:].reshape(1, B, keep, HEADS, HEAD_DIM)]
    zeros_tab = jnp.zeros((N_MEM, LANES), F32)
    w_mem = w_mem_kv[i].astype(BF16)
    (mk, mv), _ = _project(
        mem_prompt, norm_mem[i][None, :], zeros_tab, zeros_tab,
        [("lin", w_mem[:, :hw], 1.0, F32, 0, False, 1), ("lin", w_mem[:, hw:], 1.0, F32, 0, False, 1)],
        tm=N_MEM)
    ox = _cross_attention(xq, mem_rows(mk), mem_rows(mv), tq=512, nb=1)
    y_prompt, (r_d1k, r_d1v), (r_d3k,) = _finish(
        x_prompt.reshape(B * L, D), a_p, o_groups, lse_groups, ox, gates, post, 512, dils_p,
        merge_rolls=[d1k, d1v], ffn_rolls=[d3k])
    y_prompt = y_prompt.reshape(B, L, D)
    bufs_s = [c.reshape(1, SB, c.shape[1] // HEADS, HEADS, HEAD_DIM)
              for c in (r_d1k, r_d1v, r_d2k, r_d2v, r_d3k, r_d3v)]

    return (y_prompt, y_sample, state_p[None], *bufs_p,
            mk.reshape(1, B, N_MEM, HEADS, HEAD_DIM), mv.reshape(1, B, N_MEM, HEADS, HEAD_DIM),
            state_s[None], *bufs_s)
```

```python
import functools

import jax
import jax.numpy as jnp
from jax import lax
from jax.experimental import pallas as pl
from jax.experimental.pallas import tpu as pltpu

F32 = jnp.float32
BF16 = jnp.bfloat16

D_MODEL = 1024
PAST_LEN = 8192
N_MEM = 256
HEADS = 4
HEAD_DIM = 128
GLA_DV = 256
GLA_RANK = 16
GLA_GATE_NORM = 16.0
DIL_PATTERNS = ((128, 1), (512, 4), (2048, 16))
WINDOW_KEYS = 128
D_FF = 4 * D_MODEL
ROPE_THETA = 10000.0
EPS = 1e-6
NEG = -1e30
QK_SCALE = HEAD_DIM ** -0.5
LOG2_E = 1.4426950408889634

LANES = 128
COL_CHUNK = 512
VMEM_LIMIT = 48 * 1024 * 1024
VMEM_LIMIT_ROLL_HOST = 56 * 1024 * 1024


def _mm(a, b):
    return jnp.dot(a, b, preferred_element_type=F32)


def _mm_nt(a, b):
    return lax.dot_general(a, b, (((1,), (1,)), ((), ())), preferred_element_type=F32)


def _mm_tn(a, b):
    return lax.dot_general(a, b, (((0,), (0,)), ((), ())), preferred_element_type=F32)


def _rms(x, gain):
    return x * lax.rsqrt(jnp.mean(x * x, axis=-1, keepdims=True) + EPS) * gain


def _params(sem, vmem_limit=VMEM_LIMIT):
    return pltpu.CompilerParams(dimension_semantics=sem, vmem_limit_bytes=vmem_limit)


def _lane_blocks(y):
    return [y[:, j * LANES:(j + 1) * LANES] for j in range(y.shape[1] // LANES)]


def _residue_major(ref, dil):
    nblk, tm, _ = ref.shape
    per = tm // dil
    return jnp.concatenate(
        [jnp.concatenate([ref[j, pl.ds(r, per, stride=dil), :] for j in range(nblk)], axis=1)
         for r in range(dil)], axis=0)


def _store_token_order(buf, y, dil):
    per = y.shape[0] // dil
    for r in range(dil):
        for j, blk in enumerate(_lane_blocks(y[r * per:(r + 1) * per])):
            buf[j, pl.ds(r, per, stride=dil), :] = blk


def _proj_body(*refs, plan, first_tail_tile, dils):
    x_ref, g_ref, cos_ref, sin_ref, b_ref, wdec_ref, bdec_ref = refs[:7]
    w_refs = refs[7:7 + len(plan)]
    nout = sum(2 if p[4] else 1 for p in plan)
    out_refs = list(refs[7 + len(plan):7 + len(plan) + nout])
    scratch = refs[7 + len(plan) + nout:]
    tm, dm = x_ref.shape
    x = x_ref[...]
    xg = x * g_ref[...]
    inv_rms = jnp.broadcast_to(lax.rsqrt(jnp.mean(x * x, axis=-1, keepdims=True) + EPS),
                               (tm, LANES))
    if scratch:
        n_scr, tail_scr = scratch
        for j, blk in enumerate(_lane_blocks(xg) + [inv_rms]):
            n_scr[j] = blk
    rows_of = {1: xg.astype(BF16)}
    factor = {1: inv_rms}
    tables = {1: (cos_ref[...] * inv_rms, sin_ref[...] * inv_rms)}
    staged = [1]

    def prepare(d):
        held = staged[0]
        step = d // held
        assert step * held == d and step <= 4
        if held == 1:
            permuted = _residue_major(n_scr, step)
        else:
            blk = tm // held
            parts = {}
            for c in range(held):
                for s_ in range(step):
                    parts[c + held * s_] = jnp.concatenate(
                        [n_scr[j, pl.ds(c * blk + s_, blk // step, stride=step), :]
                         for j in range(n_scr.shape[0])], axis=1)
            permuted = jnp.concatenate([parts[r] for r in range(d)], axis=0)
        if d != dils[-1]:
            for j, plane in enumerate(_lane_blocks(permuted)):
                n_scr[j] = plane
            staged[0] = d
        rows_of[d] = permuted[:, :dm].astype(BF16)
        factor[d] = permuted[:, dm:]
        tables[d] = tuple(
            jnp.concatenate([t[pl.ds(r, tm // d, stride=d), :] for r in range(d)], axis=0)
            * factor[d] for t in (cos_ref, sin_ref))

    in_tail = pl.program_id(1) >= first_tail_tile
    oi = 0
    for (kind, scale, width, boff, has_tail, dil), w_ref in zip(plan, w_refs):
        out = out_refs[oi]
        oi += 1
        tail = None
        if has_tail:
            tail = out_refs[oi]
            oi += 1
        if dil not in rows_of:
            prepare(dil)
        n = rows_of[dil]
        if kind == "decay":
            ga = (_mm(n, w_ref[...]) * factor[dil]).astype(BF16)
            z = _mm(ga, wdec_ref[...]) + bdec_ref[...]
            logsig = jnp.minimum(z, 0.0) - jnp.log1p(jnp.exp(-jnp.abs(z)))
            out[...] = logsig * (1.0 / GLA_GATE_NORM)
            continue
        per = tm // dil
        for c0 in range(0, width, COL_CHUNK):
            cw = min(COL_CHUNK, width - c0)
            y = _mm(n, w_ref[:, c0:c0 + cw])
            if kind == "rope":
                cosv, sinv = tables[dil]
                heads = []
                for j in range(cw // HEAD_DIM):
                    yh = y[:, j * HEAD_DIM:(j + 1) * HEAD_DIM]
                    heads.append(yh * cosv + pltpu.roll(yh, HEAD_DIM // 2, 1) * sinv)
                y = jnp.concatenate(heads, axis=1)
            else:
                y = y * jnp.concatenate([factor[dil]] * (cw // LANES), axis=1)
            if kind == "sig":
                y = jax.nn.sigmoid(y + b_ref[:, boff + c0:boff + c0 + cw])
            if tail is not None:
                @pl.when(in_tail)
                def _():
                    if dil == 1:
                        tail[:, c0:c0 + cw] = y
                    else:
                        _store_token_order(tail_scr, y, dil)
                        for j in range(cw // LANES):
                            tail[:, c0 + j * LANES:c0 + (j + 1) * LANES] = tail_scr[j]
            if scale != 1.0:
                y = y * scale
            for r in range(dil):
                out[:, r * width + c0:r * width + c0 + cw] = y[r * per:(r + 1) * per].astype(out.dtype)


def _project(x, gain, cos, sin, segs, *, tm, tail_rows=0, bias=None, wdec=None, bdec=None,
             rolls=()):
    B, L, D = x.shape
    nL = L // tm
    ntail = tail_rows // tm
    first_tail_tile = nL - ntail
    if bias is None:
        bias = jnp.zeros((1, LANES), F32)
    if wdec is None:
        wdec = jnp.zeros((LANES, LANES), BF16)
        bdec = jnp.zeros((1, LANES), F32)

    def const2(shape):
        return pl.BlockSpec(shape, lambda b, i: (0, 0), pipeline_mode=pl.Buffered(1))

    in_specs = [
        pl.BlockSpec((None, tm, D), lambda b, i: (b, i, 0)),
        const2((1, D)),
        pl.BlockSpec((tm, LANES), lambda b, i: (i, 0)),
        pl.BlockSpec((tm, LANES), lambda b, i: (i, 0)),
        const2(bias.shape), const2(wdec.shape), const2(bdec.shape),
    ]
    plan, weights, out_shapes, out_specs = [], [], [], []
    for kind, w, scale, dtype, boff, has_tail, dil in segs:
        width = w.shape[1]
        owidth = wdec.shape[1] if kind == "decay" else width
        plan.append((kind, scale, width, boff, has_tail, dil))
        weights.append(w)
        in_specs.append(const2(w.shape))
        out_shapes.append(jax.ShapeDtypeStruct((B, L // dil, dil * owidth), dtype))
        out_specs.append(pl.BlockSpec((None, tm // dil, dil * owidth), lambda b, i: (b, i, 0)))
        if has_tail:
            out_shapes.append(jax.ShapeDtypeStruct((B, tail_rows, width), F32))
            out_specs.append(pl.BlockSpec(
                (None, tm, width),
                lambda b, i: (b, jnp.maximum(i - first_tail_tile, 0), 0),
                pipeline_mode=pl.Buffered(1)))
    dils = tuple(sorted({p[5] for p in plan}))
    scratch = []
    if dils != (1,):
        scratch = [pltpu.VMEM((D // LANES + 1, tm, LANES), F32),
                   pltpu.VMEM((COL_CHUNK // LANES, tm, LANES), F32)]
    nsteps = B * nL
    extra, extra_specs, roll_shapes, roll_specs, roll_scratch = _roll_specs(rolls, nsteps)
    body = functools.partial(_proj_body, plan=tuple(plan), first_tail_tile=first_tail_tile,
                             dils=dils)
    body = _with_rolls(body, len(in_specs), len(out_shapes), len(rolls), nsteps,
                       lambda: pl.program_id(0) * nL + pl.program_id(1))
    out = pl.pallas_call(
        body, out_shape=out_shapes + roll_shapes, grid=(B, nL),
        in_specs=in_specs + extra_specs, out_specs=out_specs + roll_specs,
        scratch_shapes=roll_scratch + scratch,
        compiler_params=_params(("arbitrary", "arbitrary"),
                                VMEM_LIMIT_ROLL_HOST if rolls else VMEM_LIMIT), name="project",
    )(x, gain, cos, sin, bias, wdec, bdec, *weights, *extra)
    return out[:len(out_shapes)], out[len(out_shapes):]


GLA_TILE = 8
GLA_SUB = 4


def _gla_level_masks(T):
    row = lax.broadcasted_iota(jnp.int32, (T, T), 0)
    col = lax.broadcasted_iota(jnp.int32, (T, T), 1)
    rid = lax.broadcasted_iota(jnp.int32, (T, 1), 0)
    levels = []
    bs = 2 * GLA_SUB
    while bs <= T:
        half = bs // 2
        second = (rid & (bs - 1)) >= half
        pair = ((row & -bs) == (col & -bs)) & ((row & (bs - 1)) >= half) & ((col & (bs - 1)) < half)
        levels.append((second, pair, bs))
        bs *= 2
    return levels


def _gla_matmuls(qf, kf, v, cum, s0, levels):
    T = qf.shape[0]
    last = cum[T - 1:T, :]
    o_inter = _mm((qf * jnp.exp2(cum)).astype(BF16), s0.astype(BF16))
    off = jnp.zeros((T, T), F32)
    for second, pair_mask, bs in levels:
        bound = []
        for j in range(T // bs):
            mid_row = j * bs + bs // 2 - 1
            mid = jnp.broadcast_to(cum[mid_row:mid_row + 1, :], (GLA_TILE, HEAD_DIM))
            bound += [mid] * (bs // GLA_TILE)
        rel = cum - jnp.concatenate(bound, axis=0)
        e = jnp.exp2(jnp.where(second, rel, -rel))
        off = jnp.where(pair_mask, _mm_nt((qf * e).astype(BF16), (kf * e).astype(BF16)), off)
    upd = _mm_tn((kf * jnp.exp2(last - cum)).astype(BF16), v)
    er = lax.broadcasted_iota(jnp.int32, (HEAD_DIM, HEAD_DIM), 0)
    ec = lax.broadcasted_iota(jnp.int32, (HEAD_DIM, HEAD_DIM), 1)
    decay_col = jnp.sum(jnp.where(er == ec, jnp.exp2(last), 0.0), axis=1, keepdims=True)
    return o_inter, off, decay_col * s0 + upd


def _gla_diagonal(qf, kf, cum, off, diag_masks):
    T = qf.shape[0]
    tiles = (T // GLA_TILE, GLA_TILE, HEAD_DIM)
    q3, k3, c3 = qf.reshape(tiles), kf.reshape(tiles), cum.reshape(tiles)
    att3 = off.reshape(T // GLA_TILE, GLA_TILE, T)
    for j in range(GLA_SUB):
        if j == 0:
            prod = q3 * k3
        else:
            prod = q3 * pltpu.roll(k3, j, 1) * jnp.exp2(c3 - pltpu.roll(c3, j, 1))
        att3 = jnp.where(diag_masks[j], jnp.sum(prod, axis=-1, keepdims=True), att3)
    return att3.reshape(T, T)


def _gla_body(q_ref, k_ref, v_ref, lf_ref, r_ref, s0_ref, hn_ref, a_ref, s_ref, *, nb):
    T = q_ref.shape[1]
    hw = HEADS * HEAD_DIM
    first = pl.program_id(1) == 0
    row = lax.broadcasted_iota(jnp.int32, (T, T), 0)
    col = lax.broadcasted_iota(jnp.int32, (T, T), 1)
    tri = jnp.where(row >= col, 1.0, 0.0).astype(BF16)
    levels = _gla_level_masks(T)
    shape3 = (T // GLA_TILE, GLA_TILE, T)
    sub_row = lax.broadcasted_iota(jnp.int32, shape3, 1)
    token = lax.broadcasted_iota(jnp.int32, shape3, 0) * GLA_TILE + sub_row
    key = lax.broadcasted_iota(jnp.int32, shape3, 2)
    diag_masks = [(key == token - j) & ((sub_row & (GLA_SUB - 1)) >= j) for j in range(GLA_SUB)]

    @pl.when(first)
    def _():
        s_ref[...] = s0_ref[...]

    cums = []
    for b in range(nb):
        lf = lf_ref[b] * LOG2_E
        hi = lf.astype(BF16)
        lo = (lf - hi.astype(F32)).astype(BF16)
        parts = _mm(tri, jnp.concatenate([hi, lo], axis=1))
        cums.append(parts[:, :hw] + parts[:, hw:])
    pairs = [(b, h) for b in range(nb) for h in range(HEADS)]
    dk = lambda h: slice(h * HEAD_DIM, (h + 1) * HEAD_DIM)
    dv = lambda h: slice(h * GLA_DV, (h + 1) * GLA_DV)
    qk = {(b, h): (q_ref[b, :, dk(h)].astype(F32), k_ref[b, :, dk(h)].astype(F32), cums[b][:, dk(h)])
          for b, h in pairs}
    stage1 = {}
    for b, h in pairs:
        qf, kf, cum = qk[b, h]
        stage1[b, h] = _gla_matmuls(qf, kf, v_ref[b, :, dv(h)], cum, s_ref[b, h], levels)
        s_ref[b, h] = stage1[b, h][2]
    att = {}
    for b, h in pairs:
        qf, kf, cum = qk[b, h]
        att[b, h] = _gla_diagonal(qf, kf, cum, stage1[b, h][1], diag_masks).astype(BF16)
    for b, h in pairs:
        o = stage1[b, h][0] + _mm(att[b, h], v_ref[b, :, dv(h)])
        r = r_ref[b, :, dv(h)].astype(F32)
        a_ref[b, :, dv(h)] = (_rms(o, hn_ref[...]) * (r * jax.nn.sigmoid(r))).astype(a_ref.dtype)


def _gla(q, k, v, lf, r, s0, head_norm, *, chunk, nb):
    B, L, hw = q.shape
    tok = lambda w: pl.BlockSpec((nb, chunk, w), lambda b, c: (b, c, 0))
    state = pl.BlockSpec((nb, HEADS, HEAD_DIM, GLA_DV), lambda b, c: (b, 0, 0, 0))
    return pl.pallas_call(
        functools.partial(_gla_body, nb=nb),
        out_shape=[jax.ShapeDtypeStruct((B, L, HEADS * GLA_DV), BF16),
                   jax.ShapeDtypeStruct(s0.shape, F32)],
        grid=(B // nb, L // chunk),
        in_specs=[tok(512), tok(512), tok(1024), tok(512), tok(1024), state,
                  pl.BlockSpec((1, GLA_DV), lambda b, c: (0, 0))],
        out_specs=[tok(1024), state],
        compiler_params=_params(("parallel", "arbitrary")), name="gla",
    )(q, k, v, lf, r, s0, head_norm)


def _band_body(q_ref, k_ref, v_ref, kp_ref, vp_ref, o_ref, lse_ref, *, tq):
    step = pl.program_id(2)
    nblk = tq // WINDOW_KEYS
    t = lax.broadcasted_iota(jnp.int32, (WINDOW_KEYS, 2 * WINDOW_KEYS), 0)
    c = lax.broadcasted_iota(jnp.int32, (WINDOW_KEYS, 2 * WINDOW_KEYS), 1)
    band = (c >= t) & (c <= t + WINDOW_KEYS)
    lane = lax.broadcasted_iota(jnp.int32, (WINDOW_KEYS, LANES), 1)
    rows = lambda j: slice(j * WINDOW_KEYS, (j + 1) * WINDOW_KEYS)
    hd = lambda h: slice(h * HEAD_DIM, (h + 1) * HEAD_DIM)

    def window(cur_ref, prev_ref, j, h):
        before = prev_ref[:, hd(h)] if j == 0 else cur_ref[rows(j - 1), hd(h)]
        return jnp.concatenate([before, cur_ref[rows(j), hd(h)]], axis=0)

    pairs = [(j, h) for j in range(nblk) for h in range(HEADS)]
    scores = [_mm_nt(q_ref[rows(j), hd(h)], window(k_ref, kp_ref, j, h)) for j, h in pairs]
    probs = []
    for (j, h), s in zip(pairs, scores):
        first_key = step * tq + (j - 1) * WINDOW_KEYS
        s = jnp.where(band & (c + first_key >= 0), s, NEG)
        m = jnp.max(s, axis=-1, keepdims=True)
        p = jnp.exp(s - m)
        den = jnp.sum(p, axis=-1, keepdims=True)
        probs.append((p.astype(BF16), den, m + jnp.log(den)))
    lse_blk = [jnp.zeros((WINDOW_KEYS, LANES), F32) for _ in range(nblk)]
    for (j, h), (p, den, lse) in zip(pairs, probs):
        o_ref[rows(j), hd(h)] = (_mm(p, window(v_ref, vp_ref, j, h)) / den).astype(o_ref.dtype)
        lse_blk[j] = jnp.where(lane == h, lse, lse_blk[j])
    for j in range(nblk):
        lse_ref[rows(j), :] = lse_blk[j][:, :HEADS]


def _band_attention(q, k, v, dil, *, tq):
    B, n, _ = q.shape
    W = HEADS * HEAD_DIM
    per = tq // WINDOW_KEYS
    cur = pl.BlockSpec((None, tq, W), lambda b, r, i: (b, i, r))
    prv = pl.BlockSpec((None, WINDOW_KEYS, W), lambda b, r, i: (b, jnp.maximum(i * per - 1, 0), r))
    o, lse = pl.pallas_call(
        functools.partial(_band_body, tq=tq),
        out_shape=[jax.ShapeDtypeStruct((B, n, dil * W), BF16),
                   jax.ShapeDtypeStruct((B, dil, n, HEADS), F32)],
        grid=(B, dil, n // tq),
        in_specs=[cur, cur, cur, prv, prv],
        out_specs=[cur, pl.BlockSpec((None, None, tq, HEADS), lambda b, r, i: (b, r, i, 0))],
        compiler_params=_params(("parallel", "parallel", "arbitrary")), name="band_attention",
    )(q, k, v, k, v)
    return o, lse.transpose(0, 2, 1, 3).reshape(B, n * dil, HEADS)


def _cross_body(q_ref, mk_ref, mv_ref, o_ref, *, nb):
    slots = mk_ref.shape[1] // HEADS
    pairs = [(b, h) for b in range(nb) for h in range(HEADS)]
    head_rows = lambda h: pl.ds(h, slots, stride=HEADS)
    lanes = lambda h: slice(h * HEAD_DIM, (h + 1) * HEAD_DIM)
    scores = [_mm_nt(q_ref[b, :, lanes(h)], mk_ref[b, head_rows(h), :].astype(BF16))
              for b, h in pairs]
    probs = []
    for s in scores:
        p = jnp.exp(s - jnp.max(s, axis=-1, keepdims=True))
        probs.append((p.astype(BF16), jnp.sum(p, axis=-1, keepdims=True)))
    for (b, h), (p, den) in zip(pairs, probs):
        pv = _mm(p, mv_ref[b, head_rows(h), :].astype(BF16))
        o_ref[b, :, lanes(h)] = (pv / den).astype(o_ref.dtype)


def _cross_attention(q, mk, mv, *, tq, nb):
    B, L, W = q.shape
    tok = pl.BlockSpec((nb, tq, W), lambda b, i: (b, i, 0))
    mem = pl.BlockSpec((nb,) + mk.shape[1:], lambda b, i: (b, 0, 0))
    return pl.pallas_call(
        functools.partial(_cross_body, nb=nb), out_shape=jax.ShapeDtypeStruct((B, L, W), BF16),
        grid=(B // nb, L // tq), in_specs=[tok, mem, mem], out_specs=tok,
        compiler_params=_params(("parallel", "parallel")), name="cross_attention",
    )(q, mk, mv)


DECODE_ROW_GROUP = 16


DECODE_ROWS = 16


def _decode_body(q_ref, kn_ref, vn_ref, ck_ref, cv_ref, o_ref, lse_ref, *, dil, nnew, nb):
    grouped = len(ck_ref.shape) == 4
    nk = (nnew if grouped else dil) * WINDOW_KEYS
    qrow = lax.broadcasted_iota(jnp.int32, (DECODE_ROWS, nk), 0)
    kcol = lax.broadcasted_iota(jnp.int32, (DECODE_ROWS, nk), 1)
    if dil == 1:
        visible = kcol >= qrow
    elif grouped:
        visible = (kcol // WINDOW_KEYS) == qrow
    else:
        visible = (kcol % dil) == qrow
    new_q = lax.broadcasted_iota(jnp.int32, (nnew, 1), 0)
    head_new = lambda h: pl.ds(h, nnew, stride=HEADS)

    def cached(ref, b, h):
        if grouped:
            return jnp.concatenate([ref[b, :, c * HEADS + h, :] for c in range(nnew)], axis=0)
        return ref[b, pl.ds(h, nk, stride=HEADS), :]

    pairs = [(b, h) for b in range(nb) for h in range(HEADS)]
    pad = jnp.zeros((DECODE_ROWS - nnew, LANES), F32)
    queries = {bh: q_ref[bh[0], head_new(bh[1]), :] for bh in pairs}
    scores = {bh: _mm_nt(jnp.concatenate([queries[bh], pad], axis=0).astype(BF16),
                         cached(ck_ref, *bh).astype(BF16)) for bh in pairs}
    soft = {}
    for b, h in pairs:
        s = jnp.where(visible, scores[b, h], NEG)[:nnew]
        kn = kn_ref[b, head_new(h), :]
        sn = []
        for c in range(nnew):
            ok = (new_q >= c) if dil == 1 else (new_q == c)
            sn.append(jnp.where(ok, jnp.sum(queries[b, h] * kn[c:c + 1, :], axis=-1, keepdims=True),
                                NEG))
        m = jnp.max(s, axis=-1, keepdims=True)
        for t in sn:
            m = jnp.maximum(m, t)
        p = jnp.exp(s - m)
        pn = [jnp.exp(t - m) for t in sn]
        den = jnp.sum(p, axis=-1, keepdims=True)
        for t in pn:
            den = den + t
        p_rows = jnp.concatenate([p, jnp.zeros((DECODE_ROWS - nnew, nk), F32)], axis=0)
        soft[b, h] = (p_rows.astype(BF16), pn, den, m + jnp.log(den))
    for b, h in pairs:
        p, pn, den, lse = soft[b, h]
        acc = _mm(p, cached(cv_ref, b, h).astype(BF16))[:nnew]
        vn = vn_ref[b, head_new(h), :]
        for c in range(nnew):
            acc = acc + pn[c] * vn[c:c + 1, :]
        o_ref[b, head_new(h), :] = acc / den
        lse_ref[b, head_new(h), :] = jnp.broadcast_to(lse, (nnew, LANES))


def _decode_attention(q, kn, vn, ck, cv, window, dil, *, nb):
    B, nrow, _ = q.shape
    rows = window * HEADS
    small = pl.BlockSpec((nb, nrow, LANES), lambda b: (b, 0, 0))
    slot_rows = HEADS * dil
    if slot_rows > DECODE_ROW_GROUP:
        view = lambda c: c.reshape(B, WINDOW_KEYS, slot_rows, LANES)
        big = pl.BlockSpec((nb, WINDOW_KEYS, DECODE_ROW_GROUP, LANES), lambda b: (b, 0, 0, 0))
    else:
        view = lambda c: c
        big = pl.BlockSpec((nb, rows, LANES), lambda b: (b, 0, 0))
    return pl.pallas_call(
        functools.partial(_decode_body, dil=dil, nnew=nrow // HEADS, nb=nb),
        out_shape=[jax.ShapeDtypeStruct((B, nrow, LANES), F32),
                   jax.ShapeDtypeStruct((B, nrow, LANES), F32)],
        grid=(B // nb,), in_specs=[small, small, small, big, big],
        out_specs=[small, small],
        compiler_params=_params(("parallel",)), name="decode_attention",
    )(q, kn, vn, view(ck), view(cv))


def _roll_copies(step, nsteps, old, new, out, buf, sems):
    total, rows, _ = old.shape
    shift = new.shape[1]
    per = total // nsteps
    batches = pl.ds(step * per, per)
    slot = step % 2
    stage = buf.at[slot]
    reads = [
        pltpu.make_async_copy(old.at[batches, pl.ds(shift, rows - shift), :],
                              stage.at[:, pl.ds(0, rows - shift), :], sems.at[slot, 0]),
        pltpu.make_async_copy(new.at[batches], stage.at[:, pl.ds(rows - shift, shift), :],
                              sems.at[slot, 1]),
    ]
    return reads, pltpu.make_async_copy(stage, out.at[batches], sems.at[slot, 2])


def _with_rolls(body, nfixed_in, nfixed_out, nrolls, nsteps, step_fn):
    if nrolls == 0:
        return body

    def wrapped(*refs):
        ins = refs[:nfixed_in]
        pairs = refs[nfixed_in:nfixed_in + 2 * nrolls]
        base = nfixed_in + 2 * nrolls
        outs = refs[base:base + nfixed_out]
        rolled = refs[base + nfixed_out:base + nfixed_out + nrolls]
        staging = refs[base + nfixed_out + nrolls:base + nfixed_out + 3 * nrolls]
        rest = refs[base + nfixed_out + 3 * nrolls:]
        step = step_fn()
        args = [(pairs[2 * t], pairs[2 * t + 1], rolled[t], staging[2 * t + 1], staging[2 * t])
                for t in range(nrolls)]

        @pl.when(step >= 2)
        def _():
            for a in args:
                _roll_copies(step - 2, nsteps, *a)[1].wait()

        current = [_roll_copies(step, nsteps, *a) for a in args]
        for reads, _ in current:
            for c in reads:
                c.start()
        body(*ins, *outs, *rest)
        for reads, write in current:
            for c in reads:
                c.wait()
            write.start()

        @pl.when(step == nsteps - 1)
        def _():
            for a in args:
                _roll_copies(step - 1, nsteps, *a)[1].wait()
            for _, write in current:
                write.wait()

    return wrapped


def _roll_specs(rolls, nsteps):
    any_spec = pl.BlockSpec(memory_space=pl.ANY)
    extra, shapes, scratch = [], [], []
    for old, new in rolls:
        assert old.shape[0] % nsteps == 0 and nsteps >= 2
        extra += [old, new]
        shapes.append(jax.ShapeDtypeStruct(old.shape, old.dtype))
        scratch += [pltpu.SemaphoreType.DMA((2, 3)),
                    pltpu.VMEM((2, old.shape[0] // nsteps) + old.shape[1:], old.dtype)]
    return extra, [any_spec] * len(extra), shapes, [any_spec] * len(shapes), scratch


def _merge_body(x_ref, a_ref, o1_ref, o2_ref, o3_ref, l1_ref, l2_ref, l3_ref, ox_ref, g_ref,
                wa_ref, wb_ref, wc_ref, wo_ref, h_ref, *scratch, dils):
    tm = x_ref.shape[0]
    hw = HEADS * HEAD_DIM
    groups, si = [], 0
    for o_ref, d in zip((o1_ref, o2_ref, o3_ref), dils):
        if d == 1:
            groups.append([o_ref[:, h * HEAD_DIM:(h + 1) * HEAD_DIM] for h in range(HEADS)])
            continue
        buf = scratch[si]
        si += 1
        for r in range(d):
            for h in range(HEADS):
                lanes = slice(r * hw + h * HEAD_DIM, r * hw + (h + 1) * HEAD_DIM)
                buf[h, pl.ds(r, tm // d, stride=d), :] = o_ref[:, lanes].astype(F32)
        groups.append([buf[h] for h in range(HEADS)])
    l1, l2, l3 = l1_ref[...], l2_ref[...], l3_ref[...]
    lmax = jnp.maximum(jnp.maximum(l1, l2), l3)
    e1, e2, e3 = jnp.exp(l1 - lmax), jnp.exp(l2 - lmax), jnp.exp(l3 - lmax)
    inv = 1.0 / (e1 + e2 + e3)
    heads = []
    for h in range(HEADS):
        heads.append((e1[:, h:h + 1] * inv[:, h:h + 1]) * groups[0][h].astype(F32)
                     + (e2[:, h:h + 1] * inv[:, h:h + 1]) * groups[1][h].astype(F32)
                     + (e3[:, h:h + 1] * inv[:, h:h + 1]) * groups[2][h].astype(F32))
    o_dil = jnp.concatenate(heads, axis=1).astype(BF16)
    ya = _mm(a_ref[...], wa_ref[...])
    yb = _mm(o_dil, wb_ref[...])
    yc = _mm(ox_ref[...], wc_ref[...])
    d = D_MODEL
    mix = (g_ref[:, 0:d].astype(F32) * ya + g_ref[:, d:2 * d].astype(F32) * yb
           + g_ref[:, 2 * d:3 * d].astype(F32) * yc)
    h_ref[...] = x_ref[...] + _mm(mix.astype(BF16), wo_ref[...])


def _merge(x, a, o_groups, lse_groups, ox, gates, wa, wb, wc, wo, *, tm, dils, rolls=()):
    T = x.shape[0]
    nsteps = T // tm
    hw = HEADS * HEAD_DIM
    tok = lambda w: pl.BlockSpec((tm, w), lambda i: (i, 0))
    grp = lambda d: pl.BlockSpec((tm // d, d * hw), lambda i: (i, 0))
    full = lambda w: pl.BlockSpec(w.shape, lambda i: (0, 0))
    extra, extra_specs, roll_shapes, roll_specs, roll_scratch = _roll_specs(rolls, nsteps)
    body = _with_rolls(functools.partial(_merge_body, dils=dils), 14, 1, len(rolls), nsteps,
                       lambda: pl.program_id(0))
    out = pl.pallas_call(
        body, out_shape=[jax.ShapeDtypeStruct((T, D_MODEL), F32)] + roll_shapes, grid=(nsteps,),
        in_specs=[tok(D_MODEL), tok(HEADS * GLA_DV), grp(dils[0]), grp(dils[1]), grp(dils[2]),
                  tok(HEADS), tok(HEADS), tok(HEADS), tok(hw), tok(3 * D_MODEL),
                  full(wa), full(wb), full(wc), full(wo)] + extra_specs,
        out_specs=[tok(D_MODEL)] + roll_specs,
        scratch_shapes=roll_scratch + [pltpu.VMEM((HEADS, tm, HEAD_DIM), F32)
                                       for d in dils if d > 1],
        compiler_params=_params(("arbitrary",)), name="merge",
    )(x, a, *o_groups, *lse_groups, ox, gates, wa, wb, wc, wo, *extra)
    return out[0], out[1:]


def _ffn_body(h_ref, gn_ref, wu_ref, wd_ref, gf_ref, y_ref):
    h = h_ref[...]
    n = (h * gn_ref[...]).astype(BF16)
    inv_ms = 1.0 / (jnp.mean(h * h, axis=-1, keepdims=True) + EPS)
    acc = None
    for c0 in range(0, D_FF, COL_CHUNK):
        u = jnp.maximum(_mm(n, wu_ref[:, c0:c0 + COL_CHUNK]), 0.0)
        part = _mm((u * u).astype(BF16), wd_ref[c0:c0 + COL_CHUNK, :])
        acc = part if acc is None else acc + part
    y_ref[...] = _rms(h + inv_ms * acc, gf_ref[...])


def _ffn(h, gain_ffn, wu, wd, gain_final, *, tm, rolls=()):
    T = h.shape[0]
    nsteps = T // tm
    tok = pl.BlockSpec((tm, D_MODEL), lambda i: (i, 0))
    full = lambda w: pl.BlockSpec(w.shape, lambda i: (0, 0), pipeline_mode=pl.Buffered(1))
    extra, extra_specs, roll_shapes, roll_specs, roll_scratch = _roll_specs(rolls, nsteps)
    body = _with_rolls(_ffn_body, 5, 1, len(rolls), nsteps, lambda: pl.program_id(0))
    out = pl.pallas_call(
        body, out_shape=[jax.ShapeDtypeStruct((T, D_MODEL), F32)] + roll_shapes, grid=(nsteps,),
        in_specs=[tok, full(gain_ffn), full(wu), full(wd), full(gain_final)] + extra_specs,
        out_specs=[tok] + roll_specs, scratch_shapes=roll_scratch,
        compiler_params=_params(("arbitrary",)), name="ffn",
    )(h, gain_ffn, wu, wd, gain_final, *extra)
    return out[0], out[1:]


def _rope_tables(pos):
    half = HEAD_DIM // 2
    inv = ROPE_THETA ** (-jnp.arange(half, dtype=F32) / half)
    ang = pos.astype(F32)[:, None] * inv[None, :]
    cos, sin = jnp.cos(ang), jnp.sin(ang)
    return jnp.concatenate([cos, cos], axis=-1), jnp.concatenate([-sin, sin], axis=-1)


def _split_weights(w_in, w_decay):
    hw = HEADS * HEAD_DIM
    sizes = [hw, hw, HEADS * GLA_DV, HEADS * GLA_DV, GLA_RANK] + [hw] * 9 + [hw, 3 * D_MODEL]
    offs = [0]
    for s in sizes:
        offs.append(offs[-1] + s)
    cols = [w_in[:, offs[j]:offs[j + 1]].astype(BF16) for j in range(len(sizes))]
    cols[4] = jnp.pad(cols[4], ((0, 0), (0, LANES - GLA_RANK)))
    wdec = jnp.pad(w_decay.astype(BF16), ((0, LANES - GLA_RANK), (0, 0)))
    return cols, wdec


def _layer(x, pos, tail_rows, cols, wdec, b_decay, b_gate, norm_mix, tm, dils,
           rolls=((), (), ())):
    cos, sin = _rope_tables(pos)
    gain = norm_mix[None, :]
    (gq, gk, gv, gr, lf), rolled_a = _project(
        x, gain, cos, sin,
        [("lin", cols[0], QK_SCALE, BF16, 0, False, 1), ("lin", cols[1], 1.0, BF16, 0, False, 1),
         ("lin", cols[2], 1.0, BF16, 0, False, 1), ("lin", cols[3], 1.0, BF16, 0, False, 1),
         ("decay", cols[4], 1.0, F32, 0, False, 1)],
        tm=tm, wdec=wdec, bdec=b_decay[None, :], rolls=rolls[0])
    dil_segs = []
    for g in range(3):
        dil_segs += [("rope", cols[5 + 3 * g], QK_SCALE, BF16, 0, False, dils[g]),
                     ("rope", cols[6 + 3 * g], 1.0, BF16, 0, True, dils[g]),
                     ("lin", cols[7 + 3 * g], 1.0, BF16, 0, True, dils[g])]
    dil_out, rolled_b = _project(x, gain, cos, sin, dil_segs, tm=tm, tail_rows=tail_rows,
                                 rolls=rolls[1])
    (xq, gates), rolled_c = _project(
        x, gain, cos, sin,
        [("lin", cols[14], QK_SCALE, BF16, 0, False, 1), ("sig", cols[15], 1.0, BF16, 0, False, 1)],
        tm=tm, bias=b_gate[None, :], rolls=rolls[2])
    dil = [dil_out[5 * g:5 * g + 5] for g in range(3)]
    return (gq, gk, gv, gr, lf), dil, xq, gates, (rolled_a, rolled_b, rolled_c)


def _finish(x2d, a, o_groups, lse_groups, ox, gates, post, tm, dils, merge_rolls=(),
            ffn_rolls=()):
    wa, wb, wc, wo, gain_ffn, wu, wd, gain_final = post
    flat = lambda t: t.reshape(-1, t.shape[-1])
    h, rolled_m = _merge(x2d, flat(a), [flat(t) for t in o_groups], [flat(t) for t in lse_groups],
                         flat(ox), flat(gates), wa, wb, wc, wo, tm=tm, dils=dils,
                         rolls=merge_rolls)
    y, rolled_f = _ffn(h, gain_ffn, wu, wd, gain_final, tm=tm, rolls=ffn_rolls)
    return y, rolled_m, rolled_f


def kernel(x_prompt, x_sample, mem_prompt, state_gla, cache_dil1_k, cache_dil1_v, cache_dil2_k, cache_dil2_v, cache_dil3_k, cache_dil3_v, cache_mem_k, cache_mem_v, norm_mix, w_in, b_gate, w_decay, b_decay, gla_head_norm, w_proj_gla, w_proj_dil, w_proj_x, norm_mem, w_mem_kv, w_out, norm_ffn, w_ffn_up, w_ffn_down, norm_final):
    B, L, D = x_prompt.shape
    SB, SL, _ = x_sample.shape
    depth = w_in.shape[0]
    assert depth == 1, "single trunk layer"
    i = 0
    cols, wdec = _split_weights(w_in[i], w_decay[i])
    post = (w_proj_gla[i].astype(BF16), w_proj_dil[i].astype(BF16), w_proj_x[i].astype(BF16),
            w_out[i].astype(BF16), norm_ffn[i][None, :], w_ffn_up[i].astype(BF16),
            w_ffn_down[i].astype(BF16), norm_final[None, :])
    head_norm = gla_head_norm[i][None, :]
    hw = HEADS * HEAD_DIM
    max_window = DIL_PATTERNS[-1][0]

    mem_rows = lambda t: t.reshape(t.shape[0], N_MEM * HEADS, HEAD_DIM)

    T = SB * SL
    pos_s = jnp.tile(PAST_LEN + jnp.arange(SL), SB)
    (gq, gk, gv, gr, lf), dil, xq, gates, _ = _layer(
        x_sample.reshape(1, T, D), pos_s, T, cols, wdec, b_decay[i], b_gate[i], norm_mix[i], T,
        (1, 1, 1))
    pad_rows = 16 - SL
    per_batch = lambda t: jnp.pad(t.reshape(SB, SL, t.shape[-1]), ((0, 0), (0, pad_rows), (0, 0)))
    a_s, state_s = _gla(per_batch(gq), per_batch(gk), per_batch(gv), per_batch(lf), per_batch(gr),
                        state_gla[i], head_norm, chunk=16, nb=8)
    a_s = a_s[:, :SL]
    caches = ((cache_dil1_k[i], cache_dil1_v[i]), (cache_dil2_k[i], cache_dil2_v[i]),
              (cache_dil3_k[i], cache_dil3_v[i]))
    o_groups, lse_groups, roll_pairs = [], [], []
    rows = lambda t: t.astype(F32).reshape(SB, SL * HEADS, HEAD_DIM)
    for (win, dl), (dq, dk, dk_tail, dv, dv_tail), (ck, cv), nb in zip(
            DIL_PATTERNS, dil, caches, (4, 2, 2)):
        ck, cv = (c.reshape(SB, win * HEADS, HEAD_DIM) for c in (ck, cv))
        o_g, lse_g = _decode_attention(rows(dq), rows(dk_tail), rows(dv_tail), ck, cv, win, dl,
                                       nb=nb)
        o_groups.append(o_g.reshape(SB, SL, hw).astype(BF16))
        lse_groups.append(lse_g[:, :, 0].reshape(SB, SL, HEADS))
        roll_pairs += [(ck, rows(dk_tail)), (cv, rows(dv_tail))]
    ox = _cross_attention(per_batch(xq), mem_rows(cache_mem_k[i]), mem_rows(cache_mem_v[i]),
                          tq=16, nb=8)[:, :SL]
    y_sample, _, _ = _finish(x_sample.reshape(T, D), a_s, o_groups, lse_groups, ox, gates, post,
                             T, (1, 1, 1))
    y_sample = y_sample.reshape(SB, SL, D)

    d1k, d1v, d2k, d2v, d3k, d3v = roll_pairs
    dils_p = tuple(dl for _, dl in DIL_PATTERNS)
    (gq, gk, gv, gr, lf), dil, xq, gates, ((r_d2v,), (r_d3v,), (r_d2k,)) = _layer(
        x_prompt, jnp.arange(L), max_window, cols, wdec, b_decay[i], b_gate[i], norm_mix[i], 512,
        dils_p, rolls=([d2v], [d3v], [d2k]))
    a_p, state_p = _gla(gq, gk, gv, lf, gr, jnp.zeros((B, HEADS, HEAD_DIM, GLA_DV), F32),
                        head_norm, chunk=128, nb=1)
    o_groups, lse_groups, bufs_p = [], [], []
    for (win, dl), (dq, dk, dk_tail, dv, dv_tail) in zip(DIL_PATTERNS, dil):
        o_g, lse_g = _band_attention(dq, dk, dv, dl, tq=512)
        o_groups.append(o_g)
        lse_groups.append(lse_g)
        keep = min(win, L)
        bufs_p += [dk_tail[:, max_window - keep:].reshape(1, B, keep, HEADS, HEAD_DIM),
                   dv_tail[:, max_window - keep:].reshape(1, B, keep, HEADS, HEAD_DIM)]
    zeros_tab = jnp.zeros((N_MEM, LANES), F32)
    w_mem = w_mem_kv[i].astype(BF16)
    (mk, mv), _ = _project(
        mem_prompt, norm_mem[i][None, :], zeros_tab, zeros_tab,
        [("lin", w_mem[:, :hw], 1.0, F32, 0, False, 1), ("lin", w_mem[:, hw:], 1.0, F32, 0, False, 1)],
        tm=N_MEM)
    ox = _cross_attention(xq, mem_rows(mk), mem_rows(mv), tq=512, nb=1)
    y_prompt, (r_d1k, r_d1v), (r_d3k,) = _finish(
        x_prompt.reshape(B * L, D), a_p, o_groups, lse_groups, ox, gates, post, 512, dils_p,
        merge_rolls=[d1k, d1v], ffn_rolls=[d3k])
    y_prompt = y_prompt.reshape(B, L, D)
    bufs_s = [c.reshape(1, SB, c.shape[1] // HEADS, HEADS, HEAD_DIM)
              for c in (r_d1k, r_d1v, r_d2k, r_d2v, r_d3k, r_d3v)]

    return (y_prompt, y_sample, state_p[None], *bufs_p,
            mk.reshape(1, B, N_MEM, HEADS, HEAD_DIM), mv.reshape(1, B, N_MEM, HEADS, HEAD_DIM),
            state_s[None], *bufs_s)
```

```python
import functools

import jax
import jax.numpy as jnp
from jax import lax
from jax.experimental import pallas as pl
from jax.experimental.pallas import tpu as pltpu

F32 = jnp.float32
BF16 = jnp.bfloat16

D_MODEL = 1024
PAST_LEN = 8192
N_MEM = 256
HEADS = 4
HEAD_DIM = 128
GLA_DV = 256
GLA_RANK = 16
GLA_GATE_NORM = 16.0
DIL_PATTERNS = ((128, 1), (512, 4), (2048, 16))
WINDOW_KEYS = 128
D_FF = 4 * D_MODEL
ROPE_THETA = 10000.0
EPS = 1e-6
NEG = -1e30
QK_SCALE = HEAD_DIM ** -0.5
LOG2_E = 1.4426950408889634

LANES = 128
COL_CHUNK = 512
VMEM_LIMIT = 48 * 1024 * 1024
VMEM_LIMIT_ROLL_HOST = 56 * 1024 * 1024


def _mm(a, b):
    return jnp.dot(a, b, preferred_element_type=F32)


def _mm_nt(a, b):
    return lax.dot_general(a, b, (((1,), (1,)), ((), ())), preferred_element_type=F32)


def _mm_tn(a, b):
    return lax.dot_general(a, b, (((0,), (0,)), ((), ())), preferred_element_type=F32)


def _rms(x, gain):
    return x * lax.rsqrt(jnp.mean(x * x, axis=-1, keepdims=True) + EPS) * gain


def _params(sem, vmem_limit=VMEM_LIMIT):
    return pltpu.CompilerParams(dimension_semantics=sem, vmem_limit_bytes=vmem_limit)


def _lane_blocks(y):
    return [y[:, j * LANES:(j + 1) * LANES] for j in range(y.shape[1] // LANES)]


def _residue_major(ref, dil):
    nblk, tm, _ = ref.shape
    per = tm // dil
    return jnp.concatenate(
        [jnp.concatenate([ref[j, pl.ds(r, per, stride=dil), :] for j in range(nblk)], axis=1)
         for r in range(dil)], axis=0)


def _store_token_order(buf, y, dil):
    per = y.shape[0] // dil
    for r in range(dil):
        for j, blk in enumerate(_lane_blocks(y[r * per:(r + 1) * per])):
            buf[j, pl.ds(r, per, stride=dil), :] = blk


def _proj_body(*refs, plan, first_tail_tile, dils):
    x_ref, g_ref, cos_ref, sin_ref, b_ref, wdec_ref, bdec_ref = refs[:7]
    w_refs = refs[7:7 + len(plan)]
    nout = sum(2 if p[4] else 1 for p in plan)
    out_refs = list(refs[7 + len(plan):7 + len(plan) + nout])
    scratch = refs[7 + len(plan) + nout:]
    tm, dm = x_ref.shape
    x = x_ref[...]
    xg = x * g_ref[...]
    inv_rms = jnp.broadcast_to(lax.rsqrt(jnp.mean(x * x, axis=-1, keepdims=True) + EPS),
                               (tm, LANES))
    if scratch:
        n_scr, tail_scr = scratch
        for j, blk in enumerate(_lane_blocks(xg) + [inv_rms]):
            n_scr[j] = blk
    rows_of = {1: xg.astype(BF16)}
    factor = {1: inv_rms}
    tables = {1: (cos_ref[...] * inv_rms, sin_ref[...] * inv_rms)}
    staged = [1]

    def prepare(d):
        held = staged[0]
        step = d // held
        assert step * held == d and step <= 4
        if held == 1:
            permuted = _residue_major(n_scr, step)
        else:
            blk = tm // held
            parts = {}
            for c in range(held):
                for s_ in range(step):
                    parts[c + held * s_] = jnp.concatenate(
                        [n_scr[j, pl.ds(c * blk + s_, blk // step, stride=step), :]
                         for j in range(n_scr.shape[0])], axis=1)
            permuted = jnp.concatenate([parts[r] for r in range(d)], axis=0)
        if d != dils[-1]:
            for j, plane in enumerate(_lane_blocks(permuted)):
                n_scr[j] = plane
            staged[0] = d
        rows_of[d] = permuted[:, :dm].astype(BF16)
        factor[d] = permuted[:, dm:]
        tables[d] = tuple(
            jnp.concatenate([t[pl.ds(r, tm // d, stride=d), :] for r in range(d)], axis=0)
            * factor[d] for t in (cos_ref, sin_ref))

    in_tail = pl.program_id(1) >= first_tail_tile
    oi = 0
    for (kind, scale, width, boff, has_tail, dil), w_ref in zip(plan, w_refs):
        out = out_refs[oi]
        oi += 1
        tail = None
        if has_tail:
            tail = out_refs[oi]
            oi += 1
        if dil not in rows_of:
            prepare(dil)
        n = rows_of[dil]
        if kind == "decay":
            ga = (_mm(n, w_ref[...]) * factor[dil]).astype(BF16)
            z = _mm(ga, wdec_ref[...]) + bdec_ref[...]
            logsig = jnp.minimum(z, 0.0) - jnp.log1p(jnp.exp(-jnp.abs(z)))
            out[...] = logsig * (1.0 / GLA_GATE_NORM)
            continue
        per = tm // dil
        for c0 in range(0, width, COL_CHUNK):
            cw = min(COL_CHUNK, width - c0)
            y = _mm(n, w_ref[:, c0:c0 + cw])
            if kind == "rope":
                cosv, sinv = tables[dil]
                heads = []
                for j in range(cw // HEAD_DIM):
                    yh = y[:, j * HEAD_DIM:(j + 1) * HEAD_DIM]
                    heads.append(yh * cosv + pltpu.roll(yh, HEAD_DIM // 2, 1) * sinv)
                y = jnp.concatenate(heads, axis=1)
            else:
                y = y * jnp.concatenate([factor[dil]] * (cw // LANES), axis=1)
            if kind == "sig":
                y = jax.nn.sigmoid(y + b_ref[:, boff + c0:boff + c0 + cw])
            if tail is not None:
                @pl.when(in_tail)
                def _():
                    if dil == 1:
                        tail[:, c0:c0 + cw] = y
                    else:
                        _store_token_order(tail_scr, y, dil)
                        for j in range(cw // LANES):
                            tail[:, c0 + j * LANES:c0 + (j + 1) * LANES] = tail_scr[j]
            if scale != 1.0:
                y = y * scale
            for r in range(dil):
                out[:, r * width + c0:r * width + c0 + cw] = y[r * per:(r + 1) * per].astype(out.dtype)


def _project(x, gain, cos, sin, segs, *, tm, tail_rows=0, bias=None, wdec=None, bdec=None,
             rolls=()):
    B, L, D = x.shape
    nL = L // tm
    ntail = tail_rows // tm
    first_tail_tile = nL - ntail
    if bias is None:
        bias = jnp.zeros((1, LANES), F32)
    if wdec is None:
        wdec = jnp.zeros((LANES, LANES), BF16)
        bdec = jnp.zeros((1, LANES), F32)

    def const2(shape):
        return pl.BlockSpec(shape, lambda b, i: (0, 0), pipeline_mode=pl.Buffered(1))

    in_specs = [
        pl.BlockSpec((None, tm, D), lambda b, i: (b, i, 0)),
        const2((1, D)),
        pl.BlockSpec((tm, LANES), lambda b, i: (i, 0)),
        pl.BlockSpec((tm, LANES), lambda b, i: (i, 0)),
        const2(bias.shape), const2(wdec.shape), const2(bdec.shape),
    ]
    plan, weights, out_shapes, out_specs = [], [], [], []
    for kind, w, scale, dtype, boff, has_tail, dil in segs:
        width = w.shape[1]
        owidth = wdec.shape[1] if kind == "decay" else width
        plan.append((kind, scale, width, boff, has_tail, dil))
        weights.append(w)
        in_specs.append(const2(w.shape))
        out_shapes.append(jax.ShapeDtypeStruct((B, L // dil, dil * owidth), dtype))
        out_specs.append(pl.BlockSpec((None, tm // dil, dil * owidth), lambda b, i: (b, i, 0)))
        if has_tail:
            out_shapes.append(jax.ShapeDtypeStruct((B, tail_rows, width), F32))
            out_specs.append(pl.BlockSpec(
                (None, tm, width),
                lambda b, i: (b, jnp.maximum(i - first_tail_tile, 0), 0),
                pipeline_mode=pl.Buffered(1)))
    dils = tuple(sorted({p[5] for p in plan}))
    scratch = []
    if dils != (1,):
        scratch = [pltpu.VMEM((D // LANES + 1, tm, LANES), F32),
                   pltpu.VMEM((COL_CHUNK // LANES, tm, LANES), F32)]
    nsteps = B * nL
    extra, extra_specs, roll_shapes, roll_specs, roll_scratch = _roll_specs(rolls, nsteps)
    body = functools.partial(_proj_body, plan=tuple(plan), first_tail_tile=first_tail_tile,
                             dils=dils)
    body = _with_rolls(body, len(in_specs), len(out_shapes), len(rolls), nsteps,
                       lambda: pl.program_id(0) * nL + pl.program_id(1))
    out = pl.pallas_call(
        body, out_shape=out_shapes + roll_shapes, grid=(B, nL),
        in_specs=in_specs + extra_specs, out_specs=out_specs + roll_specs,
        scratch_shapes=roll_scratch + scratch,
        compiler_params=_params(("arbitrary", "arbitrary"),
                                VMEM_LIMIT_ROLL_HOST if rolls else VMEM_LIMIT), name="project",
    )(x, gain, cos, sin, bias, wdec, bdec, *weights, *extra)
    return out[:len(out_shapes)], out[len(out_shapes):]


GLA_TILE = 8
GLA_SUB = 4


def _gla_level_masks(T):
    row = lax.broadcasted_iota(jnp.int32, (T, T), 0)
    col = lax.broadcasted_iota(jnp.int32, (T, T), 1)
    rid = lax.broadcasted_iota(jnp.int32, (T, 1), 0)
    levels = []
    bs = 2 * GLA_SUB
    while bs <= T:
        half = bs // 2
        second = (rid & (bs - 1)) >= half
        pair = ((row & -bs) == (col & -bs)) & ((row & (bs - 1)) >= half) & ((col & (bs - 1)) < half)
        levels.append((second, pair, bs))
        bs *= 2
    return levels


def _gla_matmuls(qf, kf, v, cum, s0, levels):
    T = qf.shape[0]
    last = cum[T - 1:T, :]
    o_inter = _mm((qf * jnp.exp2(cum)).astype(BF16), s0.astype(BF16))
    off = jnp.zeros((T, T), F32)
    for second, pair_mask, bs in levels:
        bound = []
        for j in range(T // bs):
            mid_row = j * bs + bs // 2 - 1
            mid = jnp.broadcast_to(cum[mid_row:mid_row + 1, :], (GLA_TILE, HEAD_DIM))
            bound += [mid] * (bs // GLA_TILE)
        rel = cum - jnp.concatenate(bound, axis=0)
        e = jnp.exp2(jnp.where(second, rel, -rel))
        off = jnp.where(pair_mask, _mm_nt((qf * e).astype(BF16), (kf * e).astype(BF16)), off)
    upd = _mm_tn((kf * jnp.exp2(last - cum)).astype(BF16), v)
    er = lax.broadcasted_iota(jnp.int32, (HEAD_DIM, HEAD_DIM), 0)
    ec = lax.broadcasted_iota(jnp.int32, (HEAD_DIM, HEAD_DIM), 1)
    decay_col = jnp.sum(jnp.where(er == ec, jnp.exp2(last), 0.0), axis=1, keepdims=True)
    return o_inter, off, decay_col * s0 + upd


def _gla_diagonal(qf, kf, cum, off, diag_masks):
    T = qf.shape[0]
    tiles = (T // GLA_TILE, GLA_TILE, HEAD_DIM)
    q3, k3, c3 = qf.reshape(tiles), kf.reshape(tiles), cum.reshape(tiles)
    att3 = off.reshape(T // GLA_TILE, GLA_TILE, T)
    for j in range(GLA_SUB):
        if j == 0:
            prod = q3 * k3
        else:
            prod = q3 * pltpu.roll(k3, j, 1) * jnp.exp2(c3 - pltpu.roll(c3, j, 1))
        att3 = jnp.where(diag_masks[j], jnp.sum(prod, axis=-1, keepdims=True), att3)
    return att3.reshape(T, T)


def _gla_body(q_ref, k_ref, v_ref, lf_ref, r_ref, s0_ref, hn_ref, a_ref, s_ref, *, nb):
    T = q_ref.shape[1]
    hw = HEADS * HEAD_DIM
    first = pl.program_id(1) == 0
    row = lax.broadcasted_iota(jnp.int32, (T, T), 0)
    col = lax.broadcasted_iota(jnp.int32, (T, T), 1)
    tri = jnp.where(row >= col, 1.0, 0.0).astype(BF16)
    levels = _gla_level_masks(T)
    shape3 = (T // GLA_TILE, GLA_TILE, T)
    sub_row = lax.broadcasted_iota(jnp.int32, shape3, 1)
    token = lax.broadcasted_iota(jnp.int32, shape3, 0) * GLA_TILE + sub_row
    key = lax.broadcasted_iota(jnp.int32, shape3, 2)
    diag_masks = [(key == token - j) & ((sub_row & (GLA_SUB - 1)) >= j) for j in range(GLA_SUB)]

    @pl.when(first)
    def _():
        s_ref[...] = s0_ref[...]

    cums = []
    for b in range(nb):
        lf = lf_ref[b] * LOG2_E
        hi = lf.astype(BF16)
        lo = (lf - hi.astype(F32)).astype(BF16)
        parts = _mm(tri, jnp.concatenate([hi, lo], axis=1))
        cums.append(parts[:, :hw] + parts[:, hw:])
    pairs = [(b, h) for b in range(nb) for h in range(HEADS)]
    dk = lambda h: slice(h * HEAD_DIM, (h + 1) * HEAD_DIM)
    dv = lambda h: slice(h * GLA_DV, (h + 1) * GLA_DV)
    qk = {(b, h): (q_ref[b, :, dk(h)].astype(F32), k_ref[b, :, dk(h)].astype(F32), cums[b][:, dk(h)])
          for b, h in pairs}
    stage1 = {}
    for b, h in pairs:
        qf, kf, cum = qk[b, h]
        stage1[b, h] = _gla_matmuls(qf, kf, v_ref[b, :, dv(h)], cum, s_ref[b, h], levels)
        s_ref[b, h] = stage1[b, h][2]
    att = {}
    for b, h in pairs:
        qf, kf, cum = qk[b, h]
        att[b, h] = _gla_diagonal(qf, kf, cum, stage1[b, h][1], diag_masks).astype(BF16)
    for b, h in pairs:
        o = stage1[b, h][0] + _mm(att[b, h], v_ref[b, :, dv(h)])
        r = r_ref[b, :, dv(h)].astype(F32)
        a_ref[b, :, dv(h)] = (_rms(o, hn_ref[...]) * (r * jax.nn.sigmoid(r))).astype(a_ref.dtype)


def _gla(q, k, v, lf, r, s0, head_norm, *, chunk, nb):
    B, L, hw = q.shape
    tok = lambda w: pl.BlockSpec((nb, chunk, w), lambda b, c: (b, c, 0))
    state = pl.BlockSpec((nb, HEADS, HEAD_DIM, GLA_DV), lambda b, c: (b, 0, 0, 0))
    return pl.pallas_call(
        functools.partial(_gla_body, nb=nb),
        out_shape=[jax.ShapeDtypeStruct((B, L, HEADS * GLA_DV), BF16),
                   jax.ShapeDtypeStruct(s0.shape, F32)],
        grid=(B // nb, L // chunk),
        in_specs=[tok(512), tok(512), tok(1024), tok(512), tok(1024), state,
                  pl.BlockSpec((1, GLA_DV), lambda b, c: (0, 0))],
        out_specs=[tok(1024), state],
        compiler_params=_params(("parallel", "arbitrary")), name="gla",
    )(q, k, v, lf, r, s0, head_norm)


def _band_body(q_ref, k_ref, v_ref, kp_ref, vp_ref, o_ref, lse_ref, *, tq):
    step = pl.program_id(2)
    nblk = tq // WINDOW_KEYS
    t = lax.broadcasted_iota(jnp.int32, (WINDOW_KEYS, 2 * WINDOW_KEYS), 0)
    c = lax.broadcasted_iota(jnp.int32, (WINDOW_KEYS, 2 * WINDOW_KEYS), 1)
    band = (c >= t) & (c <= t + WINDOW_KEYS)
    lane = lax.broadcasted_iota(jnp.int32, (WINDOW_KEYS, LANES), 1)
    rows = lambda j: slice(j * WINDOW_KEYS, (j + 1) * WINDOW_KEYS)
    hd = lambda h: slice(h * HEAD_DIM, (h + 1) * HEAD_DIM)

    def window(cur_ref, prev_ref, j, h):
        before = prev_ref[:, hd(h)] if j == 0 else cur_ref[rows(j - 1), hd(h)]
        return jnp.concatenate([before, cur_ref[rows(j), hd(h)]], axis=0)

    pairs = [(j, h) for j in range(nblk) for h in range(HEADS)]
    scores = [_mm_nt(q_ref[rows(j), hd(h)], window(k_ref, kp_ref, j, h)) for j, h in pairs]
    probs = []
    for (j, h), s in zip(pairs, scores):
        first_key = step * tq + (j - 1) * WINDOW_KEYS
        s = jnp.where(band & (c + first_key >= 0), s, NEG)
        m = jnp.max(s, axis=-1, keepdims=True)
        p = jnp.exp(s - m)
        den = jnp.sum(p, axis=-1, keepdims=True)
        probs.append((p.astype(BF16), den, m + jnp.log(den)))
    lse_blk = [jnp.zeros((WINDOW_KEYS, LANES), F32) for _ in range(nblk)]
    for (j, h), (p, den, lse) in zip(pairs, probs):
        o_ref[rows(j), hd(h)] = (_mm(p, window(v_ref, vp_ref, j, h)) / den).astype(o_ref.dtype)
        lse_blk[j] = jnp.where(lane == h, lse, lse_blk[j])
    for j in range(nblk):
        lse_ref[rows(j), :] = lse_blk[j][:, :HEADS]


def _band_attention(q, k, v, dil, *, tq):
    B, n, _ = q.shape
    W = HEADS * HEAD_DIM
    per = tq // WINDOW_KEYS
    cur = pl.BlockSpec((None, tq, W), lambda b, r, i: (b, i, r))
    prv = pl.BlockSpec((None, WINDOW_KEYS, W), lambda b, r, i: (b, jnp.maximum(i * per - 1, 0), r))
    o, lse = pl.pallas_call(
        functools.partial(_band_body, tq=tq),
        out_shape=[jax.ShapeDtypeStruct((B, n, dil * W), BF16),
                   jax.ShapeDtypeStruct((B, dil, n, HEADS), F32)],
        grid=(B, dil, n // tq),
        in_specs=[cur, cur, cur, prv, prv],
        out_specs=[cur, pl.BlockSpec((None, None, tq, HEADS), lambda b, r, i: (b, r, i, 0))],
        compiler_params=_params(("parallel", "parallel", "arbitrary")), name="band_attention",
    )(q, k, v, k, v)
    return o, lse.transpose(0, 2, 1, 3).reshape(B, n * dil, HEADS)


def _cross_body(q_ref, mk_ref, mv_ref, o_ref, *, nb):
    slots = mk_ref.shape[1] // HEADS
    pairs = [(b, h) for b in range(nb) for h in range(HEADS)]
    head_rows = lambda h: pl.ds(h, slots, stride=HEADS)
    lanes = lambda h: slice(h * HEAD_DIM, (h + 1) * HEAD_DIM)
    scores = [_mm_nt(q_ref[b, :, lanes(h)], mk_ref[b, head_rows(h), :].astype(BF16))
              for b, h in pairs]
    probs = []
    for s in scores:
        p = jnp.exp(s - jnp.max(s, axis=-1, keepdims=True))
        probs.append((p.astype(BF16), jnp.sum(p, axis=-1, keepdims=True)))
    for (b, h), (p, den) in zip(pairs, probs):
        pv = _mm(p, mv_ref[b, head_rows(h), :].astype(BF16))
        o_ref[b, :, lanes(h)] = (pv / den).astype(o_ref.dtype)


def _cross_attention(q, mk, mv, *, tq, nb):
    B, L, W = q.shape
    tok = pl.BlockSpec((nb, tq, W), lambda b, i: (b, i, 0))
    mem = pl.BlockSpec((nb,) + mk.shape[1:], lambda b, i: (b, 0, 0))
    return pl.pallas_call(
        functools.partial(_cross_body, nb=nb), out_shape=jax.ShapeDtypeStruct((B, L, W), BF16),
        grid=(B // nb, L // tq), in_specs=[tok, mem, mem], out_specs=tok,
        compiler_params=_params(("parallel", "parallel")), name="cross_attention",
    )(q, mk, mv)


DECODE_ROW_GROUP = 16


DECODE_ROWS = 16


def _decode_body(q_ref, kn_ref, vn_ref, ck_ref, cv_ref, o_ref, lse_ref, *, dil, nnew, nb):
    nk = dil * WINDOW_KEYS
    qrow = lax.broadcasted_iota(jnp.int32, (DECODE_ROWS, nk), 0)
    kcol = lax.broadcasted_iota(jnp.int32, (DECODE_ROWS, nk), 1)
    visible = (kcol >= qrow) if dil == 1 else ((kcol % dil) == qrow)
    new_q = lax.broadcasted_iota(jnp.int32, (nnew, 1), 0)
    head_new = lambda h: pl.ds(h, nnew, stride=HEADS)
    cached = lambda ref, b, h: ref[b, pl.ds(h, nk, stride=HEADS), :]

    pairs = [(b, h) for b in range(nb) for h in range(HEADS)]
    pad = jnp.zeros((DECODE_ROWS - nnew, LANES), F32)
    queries = {bh: q_ref[bh[0], head_new(bh[1]), :] for bh in pairs}
    scores = {bh: _mm_nt(jnp.concatenate([queries[bh], pad], axis=0).astype(BF16),
                         cached(ck_ref, *bh).astype(BF16)) for bh in pairs}
    soft = {}
    for b, h in pairs:
        s = jnp.where(visible, scores[b, h], NEG)[:nnew]
        kn = kn_ref[b, head_new(h), :]
        sn = []
        for c in range(nnew):
            ok = (new_q >= c) if dil == 1 else (new_q == c)
            sn.append(jnp.where(ok, jnp.sum(queries[b, h] * kn[c:c + 1, :], axis=-1, keepdims=True),
                                NEG))
        m = jnp.max(s, axis=-1, keepdims=True)
        for t in sn:
            m = jnp.maximum(m, t)
        p = jnp.exp(s - m)
        pn = [jnp.exp(t - m) for t in sn]
        den = jnp.sum(p, axis=-1, keepdims=True)
        for t in pn:
            den = den + t
        p_rows = jnp.concatenate([p, jnp.zeros((DECODE_ROWS - nnew, nk), F32)], axis=0)
        soft[b, h] = (p_rows.astype(BF16), pn, den, m + jnp.log(den))
    for b, h in pairs:
        p, pn, den, lse = soft[b, h]
        acc = _mm(p, cached(cv_ref, b, h).astype(BF16))[:nnew]
        vn = vn_ref[b, head_new(h), :]
        for c in range(nnew):
            acc = acc + pn[c] * vn[c:c + 1, :]
        o_ref[b, head_new(h), :] = acc / den
        lse_ref[b, head_new(h), :] = jnp.broadcast_to(lse, (nnew, LANES))


def _decode_grouped_body(q_ref, kn_ref, vn_ref, ck_ref, cv_ref, o_ref, lse_ref, *, nb):
    group = ck_ref.shape[2]
    nk = WINDOW_KEYS * GLA_TILE
    qrow = lax.broadcasted_iota(jnp.int32, (DECODE_ROWS, nk), 0)
    kcol = lax.broadcasted_iota(jnp.int32, (DECODE_ROWS, nk), 1)
    visible = (kcol % GLA_TILE) == qrow
    rows = lambda t: slice(t * GLA_TILE, (t + 1) * GLA_TILE)
    tile_of = lambda ref, b, t: ref[b, :, rows(t), :].reshape(nk, LANES).astype(BF16)
    units = [(b, t) for b in range(nb) for t in range(group // GLA_TILE)]
    pad = jnp.zeros((DECODE_ROWS - GLA_TILE, LANES), F32)
    scores = {(b, t): _mm_nt(jnp.concatenate([q_ref[b, rows(t), :], pad], axis=0).astype(BF16),
                             tile_of(ck_ref, b, t)) for b, t in units}
    soft = {}
    for b, t in units:
        s = jnp.where(visible, scores[b, t], NEG)[:GLA_TILE]
        sn = jnp.sum(q_ref[b, rows(t), :] * kn_ref[b, rows(t), :], axis=-1, keepdims=True)
        m = jnp.maximum(jnp.max(s, axis=-1, keepdims=True), sn)
        p = jnp.exp(s - m)
        pn = jnp.exp(sn - m)
        den = jnp.sum(p, axis=-1, keepdims=True) + pn
        p_rows = jnp.concatenate([p, jnp.zeros((DECODE_ROWS - GLA_TILE, nk), F32)], axis=0)
        soft[b, t] = (p_rows.astype(BF16), pn, den, m + jnp.log(den))
    for b, t in units:
        p, pn, den, lse = soft[b, t]
        acc = _mm(p, tile_of(cv_ref, b, t))[:GLA_TILE] + pn * vn_ref[b, rows(t), :]
        o_ref[b, rows(t), :] = acc / den
        lse_ref[b, rows(t), :] = jnp.broadcast_to(lse, (GLA_TILE, LANES))


def _decode_attention(q, kn, vn, ck, cv, window, dil, *, nb):
    B, nrow, _ = q.shape
    rows = window * HEADS
    small = pl.BlockSpec((nb, nrow, LANES), lambda b: (b, 0, 0))
    slot_rows = HEADS * dil
    if slot_rows > DECODE_ROW_GROUP:
        assert nrow == DECODE_ROW_GROUP
        view = lambda c: c.reshape(B, WINDOW_KEYS, slot_rows, LANES)
        big = pl.BlockSpec((nb, WINDOW_KEYS, DECODE_ROW_GROUP, LANES), lambda b: (b, 0, 0, 0))
        body = functools.partial(_decode_grouped_body, nb=nb)
    else:
        view = lambda c: c
        big = pl.BlockSpec((nb, rows, LANES), lambda b: (b, 0, 0))
        body = functools.partial(_decode_body, dil=dil, nnew=nrow // HEADS, nb=nb)
    return pl.pallas_call(
        body,
        out_shape=[jax.ShapeDtypeStruct((B, nrow, LANES), F32),
                   jax.ShapeDtypeStruct((B, nrow, LANES), F32)],
        grid=(B // nb,), in_specs=[small, small, small, big, big],
        out_specs=[small, small],
        compiler_params=_params(("parallel",)), name="decode_attention",
    )(q, kn, vn, view(ck), view(cv))


def _roll_copies(step, nsteps, old, new, out, buf, sems):
    total, rows, _ = old.shape
    shift = new.shape[1]
    per = total // nsteps
    batches = pl.ds(step * per, per)
    slot = step % 2
    stage = buf.at[slot]
    reads = [
        pltpu.make_async_copy(old.at[batches, pl.ds(shift, rows - shift), :],
                              stage.at[:, pl.ds(0, rows - shift), :], sems.at[slot, 0]),
        pltpu.make_async_copy(new.at[batches], stage.at[:, pl.ds(rows - shift, shift), :],
                              sems.at[slot, 1]),
    ]
    return reads, pltpu.make_async_copy(stage, out.at[batches], sems.at[slot, 2])


def _with_rolls(body, nfixed_in, nfixed_out, nrolls, nsteps, step_fn):
    if nrolls == 0:
        return body

    def wrapped(*refs):
        ins = refs[:nfixed_in]
        pairs = refs[nfixed_in:nfixed_in + 2 * nrolls]
        base = nfixed_in + 2 * nrolls
        outs = refs[base:base + nfixed_out]
        rolled = refs[base + nfixed_out:base + nfixed_out + nrolls]
        staging = refs[base + nfixed_out + nrolls:base + nfixed_out + 3 * nrolls]
        rest = refs[base + nfixed_out + 3 * nrolls:]
        step = step_fn()
        args = [(pairs[2 * t], pairs[2 * t + 1], rolled[t], staging[2 * t + 1], staging[2 * t])
                for t in range(nrolls)]

        @pl.when(step >= 2)
        def _():
            for a in args:
                _roll_copies(step - 2, nsteps, *a)[1].wait()

        current = [_roll_copies(step, nsteps, *a) for a in args]
        for reads, _ in current:
            for c in reads:
                c.start()
        body(*ins, *outs, *rest)
        for reads, write in current:
            for c in reads:
                c.wait()
            write.start()

        @pl.when(step == nsteps - 1)
        def _():
            for a in args:
                _roll_copies(step - 1, nsteps, *a)[1].wait()
            for _, write in current:
                write.wait()

    return wrapped


def _roll_specs(rolls, nsteps):
    any_spec = pl.BlockSpec(memory_space=pl.ANY)
    extra, shapes, scratch = [], [], []
    for old, new in rolls:
        assert old.shape[0] % nsteps == 0 and nsteps >= 2
        extra += [old, new]
        shapes.append(jax.ShapeDtypeStruct(old.shape, old.dtype))
        scratch += [pltpu.SemaphoreType.DMA((2, 3)),
                    pltpu.VMEM((2, old.shape[0] // nsteps) + old.shape[1:], old.dtype)]
    return extra, [any_spec] * len(extra), shapes, [any_spec] * len(shapes), scratch


def _merge_body(x_ref, a_ref, o1_ref, o2_ref, o3_ref, l1_ref, l2_ref, l3_ref, ox_ref, g_ref,
                wa_ref, wb_ref, wc_ref, wo_ref, h_ref, *scratch, dils):
    tm = x_ref.shape[0]
    hw = HEADS * HEAD_DIM
    groups, si = [], 0
    for o_ref, d in zip((o1_ref, o2_ref, o3_ref), dils):
        if d == 1:
            groups.append([o_ref[:, h * HEAD_DIM:(h + 1) * HEAD_DIM] for h in range(HEADS)])
            continue
        buf = scratch[si]
        si += 1
        for r in range(d):
            for h in range(HEADS):
                lanes = slice(r * hw + h * HEAD_DIM, r * hw + (h + 1) * HEAD_DIM)
                buf[h, pl.ds(r, tm // d, stride=d), :] = o_ref[:, lanes].astype(F32)
        groups.append([buf[h] for h in range(HEADS)])
    l1, l2, l3 = l1_ref[...], l2_ref[...], l3_ref[...]
    lmax = jnp.maximum(jnp.maximum(l1, l2), l3)
    e1, e2, e3 = jnp.exp(l1 - lmax), jnp.exp(l2 - lmax), jnp.exp(l3 - lmax)
    inv = 1.0 / (e1 + e2 + e3)
    heads = []
    for h in range(HEADS):
        heads.append((e1[:, h:h + 1] * inv[:, h:h + 1]) * groups[0][h].astype(F32)
                     + (e2[:, h:h + 1] * inv[:, h:h + 1]) * groups[1][h].astype(F32)
                     + (e3[:, h:h + 1] * inv[:, h:h + 1]) * groups[2][h].astype(F32))
    o_dil = jnp.concatenate(heads, axis=1).astype(BF16)
    ya = _mm(a_ref[...], wa_ref[...])
    yb = _mm(o_dil, wb_ref[...])
    yc = _mm(ox_ref[...], wc_ref[...])
    d = D_MODEL
    mix = (g_ref[:, 0:d].astype(F32) * ya + g_ref[:, d:2 * d].astype(F32) * yb
           + g_ref[:, 2 * d:3 * d].astype(F32) * yc)
    h_ref[...] = x_ref[...] + _mm(mix.astype(BF16), wo_ref[...])


def _merge(x, a, o_groups, lse_groups, ox, gates, wa, wb, wc, wo, *, tm, dils, rolls=()):
    T = x.shape[0]
    nsteps = T // tm
    hw = HEADS * HEAD_DIM
    tok = lambda w: pl.BlockSpec((tm, w), lambda i: (i, 0))
    grp = lambda d: pl.BlockSpec((tm // d, d * hw), lambda i: (i, 0))
    full = lambda w: pl.BlockSpec(w.shape, lambda i: (0, 0))
    extra, extra_specs, roll_shapes, roll_specs, roll_scratch = _roll_specs(rolls, nsteps)
    body = _with_rolls(functools.partial(_merge_body, dils=dils), 14, 1, len(rolls), nsteps,
                       lambda: pl.program_id(0))
    out = pl.pallas_call(
        body, out_shape=[jax.ShapeDtypeStruct((T, D_MODEL), F32)] + roll_shapes, grid=(nsteps,),
        in_specs=[tok(D_MODEL), tok(HEADS * GLA_DV), grp(dils[0]), grp(dils[1]), grp(dils[2]),
                  tok(HEADS), tok(HEADS), tok(HEADS), tok(hw), tok(3 * D_MODEL),
                  full(wa), full(wb), full(wc), full(wo)] + extra_specs,
        out_specs=[tok(D_MODEL)] + roll_specs,
        scratch_shapes=roll_scratch + [pltpu.VMEM((HEADS, tm, HEAD_DIM), F32)
                                       for d in dils if d > 1],
        compiler_params=_params(("arbitrary",)), name="merge",
    )(x, a, *o_groups, *lse_groups, ox, gates, wa, wb, wc, wo, *extra)
    return out[0], out[1:]


def _ffn_body(h_ref, gn_ref, wu_ref, wd_ref, gf_ref, y_ref):
    h = h_ref[...]
    n = (h * gn_ref[...]).astype(BF16)
    inv_ms = 1.0 / (jnp.mean(h * h, axis=-1, keepdims=True) + EPS)
    acc = None
    for c0 in range(0, D_FF, COL_CHUNK):
        u = jnp.maximum(_mm(n, wu_ref[:, c0:c0 + COL_CHUNK]), 0.0)
        part = _mm((u * u).astype(BF16), wd_ref[c0:c0 + COL_CHUNK, :])
        acc = part if acc is None else acc + part
    y_ref[...] = _rms(h + inv_ms * acc, gf_ref[...])


def _ffn(h, gain_ffn, wu, wd, gain_final, *, tm, rolls=()):
    T = h.shape[0]
    nsteps = T // tm
    tok = pl.BlockSpec((tm, D_MODEL), lambda i: (i, 0))
    full = lambda w: pl.BlockSpec(w.shape, lambda i: (0, 0), pipeline_mode=pl.Buffered(1))
    extra, extra_specs, roll_shapes, roll_specs, roll_scratch = _roll_specs(rolls, nsteps)
    body = _with_rolls(_ffn_body, 5, 1, len(rolls), nsteps, lambda: pl.program_id(0))
    out = pl.pallas_call(
        body, out_shape=[jax.ShapeDtypeStruct((T, D_MODEL), F32)] + roll_shapes, grid=(nsteps,),
        in_specs=[tok, full(gain_ffn), full(wu), full(wd), full(gain_final)] + extra_specs,
        out_specs=[tok] + roll_specs, scratch_shapes=roll_scratch,
        compiler_params=_params(("arbitrary",)), name="ffn",
    )(h, gain_ffn, wu, wd, gain_final, *extra)
    return out[0], out[1:]


def _rope_angles(pos):
    half = HEAD_DIM // 2
    inv = ROPE_THETA ** (-jnp.arange(half, dtype=F32) / half)
    ang = pos.astype(F32)[:, None] * inv[None, :]
    return jnp.cos(ang), jnp.sin(ang)


def _rope_tables(pos):
    cos, sin = _rope_angles(pos)
    return jnp.concatenate([cos, cos], axis=-1), jnp.concatenate([-sin, sin], axis=-1)


def _rope_tables_range(length, block=LANES):
    cos_hi, sin_hi = _rope_angles(jnp.arange(0, length, block))
    cos_lo, sin_lo = _rope_angles(jnp.arange(block))
    half = HEAD_DIM // 2
    cos = (cos_hi[:, None] * cos_lo[None] - sin_hi[:, None] * sin_lo[None]).reshape(length, half)
    sin = (sin_hi[:, None] * cos_lo[None] + cos_hi[:, None] * sin_lo[None]).reshape(length, half)
    return jnp.concatenate([cos, cos], axis=-1), jnp.concatenate([-sin, sin], axis=-1)


def _split_weights(w_in, w_decay):
    hw = HEADS * HEAD_DIM
    sizes = [hw, hw, HEADS * GLA_DV, HEADS * GLA_DV, GLA_RANK] + [hw] * 9 + [hw, 3 * D_MODEL]
    offs = [0]
    for s in sizes:
        offs.append(offs[-1] + s)
    cols = [w_in[:, offs[j]:offs[j + 1]].astype(BF16) for j in range(len(sizes))]
    cols[4] = jnp.pad(cols[4], ((0, 0), (0, LANES - GLA_RANK)))
    wdec = jnp.pad(w_decay.astype(BF16), ((0, LANES - GLA_RANK), (0, 0)))
    return cols, wdec


def _layer(x, rope, tail_rows, cols, wdec, b_decay, b_gate, norm_mix, tm, dils,
           rolls=((), (), ())):
    cos, sin = rope
    gain = norm_mix[None, :]
    (gq, gk, gv, gr, lf), rolled_a = _project(
        x, gain, cos, sin,
        [("lin", cols[0], QK_SCALE, BF16, 0, False, 1), ("lin", cols[1], 1.0, BF16, 0, False, 1),
         ("lin", cols[2], 1.0, BF16, 0, False, 1), ("lin", cols[3], 1.0, BF16, 0, False, 1),
         ("decay", cols[4], 1.0, F32, 0, False, 1)],
        tm=tm, wdec=wdec, bdec=b_decay[None, :], rolls=rolls[0])
    dil_segs = []
    for g in range(3):
        dil_segs += [("rope", cols[5 + 3 * g], QK_SCALE, BF16, 0, False, dils[g]),
                     ("rope", cols[6 + 3 * g], 1.0, BF16, 0, True, dils[g]),
                     ("lin", cols[7 + 3 * g], 1.0, BF16, 0, True, dils[g])]
    dil_out, rolled_b = _project(x, gain, cos, sin, dil_segs, tm=tm, tail_rows=tail_rows,
                                 rolls=rolls[1])
    (xq, gates), rolled_c = _project(
        x, gain, cos, sin,
        [("lin", cols[14], QK_SCALE, BF16, 0, False, 1), ("sig", cols[15], 1.0, BF16, 0, False, 1)],
        tm=tm, bias=b_gate[None, :], rolls=rolls[2])
    dil = [dil_out[5 * g:5 * g + 5] for g in range(3)]
    return (gq, gk, gv, gr, lf), dil, xq, gates, (rolled_a, rolled_b, rolled_c)


def _finish(x2d, a, o_groups, lse_groups, ox, gates, post, tm, dils, merge_rolls=(),
            ffn_rolls=()):
    wa, wb, wc, wo, gain_ffn, wu, wd, gain_final = post
    flat = lambda t: t.reshape(-1, t.shape[-1])
    h, rolled_m = _merge(x2d, flat(a), [flat(t) for t in o_groups], [flat(t) for t in lse_groups],
                         flat(ox), flat(gates), wa, wb, wc, wo, tm=tm, dils=dils,
                         rolls=merge_rolls)
    y, rolled_f = _ffn(h, gain_ffn, wu, wd, gain_final, tm=tm, rolls=ffn_rolls)
    return y, rolled_m, rolled_f


def kernel(x_prompt, x_sample, mem_prompt, state_gla, cache_dil1_k, cache_dil1_v, cache_dil2_k, cache_dil2_v, cache_dil3_k, cache_dil3_v, cache_mem_k, cache_mem_v, norm_mix, w_in, b_gate, w_decay, b_decay, gla_head_norm, w_proj_gla, w_proj_dil, w_proj_x, norm_mem, w_mem_kv, w_out, norm_ffn, w_ffn_up, w_ffn_down, norm_final):
    B, L, D = x_prompt.shape
    SB, SL, _ = x_sample.shape
    depth = w_in.shape[0]
    assert depth == 1, "single trunk layer"
    i = 0
    cols, wdec = _split_weights(w_in[i], w_decay[i])
    post = (w_proj_gla[i].astype(BF16), w_proj_dil[i].astype(BF16), w_proj_x[i].astype(BF16),
            w_out[i].astype(BF16), norm_ffn[i][None, :], w_ffn_up[i].astype(BF16),
            w_ffn_down[i].astype(BF16), norm_final[None, :])
    head_norm = gla_head_norm[i][None, :]
    hw = HEADS * HEAD_DIM
    max_window = DIL_PATTERNS[-1][0]

    mem_rows = lambda t: t.reshape(t.shape[0], N_MEM * HEADS, HEAD_DIM)

    T = SB * SL
    pos_s = jnp.tile(PAST_LEN + jnp.arange(SL), SB)
    (gq, gk, gv, gr, lf), dil, xq, gates, _ = _layer(
        x_sample.reshape(1, T, D), _rope_tables(pos_s), T, cols, wdec, b_decay[i], b_gate[i],
        norm_mix[i], T, (1, 1, 1))
    pad_rows = 16 - SL
    per_batch = lambda t: jnp.pad(t.reshape(SB, SL, t.shape[-1]), ((0, 0), (0, pad_rows), (0, 0)))
    a_s, state_s = _gla(per_batch(gq), per_batch(gk), per_batch(gv), per_batch(lf), per_batch(gr),
                        state_gla[i], head_norm, chunk=16, nb=8)
    a_s = a_s[:, :SL]
    caches = ((cache_dil1_k[i], cache_dil1_v[i]), (cache_dil2_k[i], cache_dil2_v[i]),
              (cache_dil3_k[i], cache_dil3_v[i]))
    o_groups, lse_groups, roll_pairs = [], [], []
    rows = lambda t: t.astype(F32).reshape(SB, SL * HEADS, HEAD_DIM)
    for (win, dl), (dq, dk, dk_tail, dv, dv_tail), (ck, cv), nb in zip(
            DIL_PATTERNS, dil, caches, (4, 4, 4)):
        ck, cv = (c.reshape(SB, win * HEADS, HEAD_DIM) for c in (ck, cv))
        o_g, lse_g = _decode_attention(rows(dq), rows(dk_tail), rows(dv_tail), ck, cv, win, dl,
                                       nb=nb)
        o_groups.append(o_g.reshape(SB, SL, hw).astype(BF16))
        lse_groups.append(lse_g[:, :, 0].reshape(SB, SL, HEADS))
        roll_pairs += [(ck, rows(dk_tail)), (cv, rows(dv_tail))]
    ox = _cross_attention(per_batch(xq), mem_rows(cache_mem_k[i]), mem_rows(cache_mem_v[i]),
                          tq=16, nb=8)[:, :SL]
    y_sample, _, _ = _finish(x_sample.reshape(T, D), a_s, o_groups, lse_groups, ox, gates, post,
                             T, (1, 1, 1))
    y_sample = y_sample.reshape(SB, SL, D)

    d1k, d1v, d2k, d2v, d3k, d3v = roll_pairs
    dils_p = tuple(dl for _, dl in DIL_PATTERNS)
    (gq, gk, gv, gr, lf), dil, xq, gates, ((r_d2v,), (r_d3v,), (r_d2k,)) = _layer(
        x_prompt, _rope_tables_range(L), max_window, cols, wdec, b_decay[i], b_gate[i],
        norm_mix[i], 512, dils_p, rolls=([d2v], [d3v], [d2k]))
    a_p, state_p = _gla(gq, gk, gv, lf, gr, jnp.zeros((B, HEADS, HEAD_DIM, GLA_DV), F32),
                        head_norm, chunk=128, nb=1)
    o_groups, lse_groups, bufs_p = [], [], []
    for (win, dl), (dq, dk, dk_tail, dv, dv_tail) in zip(DIL_PATTERNS, dil):
        o_g, lse_g = _band_attention(dq, dk, dv, dl, tq=512)
        o_groups.append(o_g)
        lse_groups.append(lse_g)
        keep = min(win, L)
        bufs_p += [dk_tail[:, max_window - keep:].reshape(1, B, keep, HEADS, HEAD_DIM),
                   dv_tail[:, max_window - keep:].reshape(1, B, keep, HEADS, HEAD_DIM)]
    zeros_tab = jnp.zeros((N_MEM, LANES), F32)
    w_mem = w_mem_kv[i].astype(BF16)
    (mk, mv), _ = _project(
        mem_prompt, norm_mem[i][None, :], zeros_tab, zeros_tab,
        [("lin", w_mem[:, :hw], 1.0, F32, 0, False, 1), ("lin", w_mem[:, hw:], 1.0, F32, 0, False, 1)],
        tm=N_MEM)
    ox = _cross_attention(xq, mem_rows(mk), mem_rows(mv), tq=512, nb=1)
    y_prompt, (r_d1k, r_d1v), (r_d3k,) = _finish(
        x_prompt.reshape(B * L, D), a_p, o_groups, lse_groups, ox, gates, post, 512, dils_p,
        merge_rolls=[d1k, d1v], ffn_rolls=[d3k])
    y_prompt = y_prompt.reshape(B, L, D)
    bufs_s = [c.reshape(1, SB, c.shape[1] // HEADS, HEADS, HEAD_DIM)
              for c in (r_d1k, r_d1v, r_d2k, r_d2v, r_d3k, r_d3v)]

    return (y_prompt, y_sample, state_p[None], *bufs_p,
            mk.reshape(1, B, N_MEM, HEADS, HEAD_DIM), mv.reshape(1, B, N_MEM, HEADS, HEAD_DIM),
            state_s[None], *bufs_s)
```

```python
import functools

import jax
import jax.numpy as jnp
from jax import lax
from jax.experimental import pallas as pl
from jax.experimental.pallas import tpu as pltpu

F32 = jnp.float32
BF16 = jnp.bfloat16

D_MODEL = 1024
PAST_LEN = 8192
N_MEM = 256
HEADS = 4
HEAD_DIM = 128
GLA_DV = 256
GLA_RANK = 16
GLA_GATE_NORM = 16.0
DIL_PATTERNS = ((128, 1), (512, 4), (2048, 16))
WINDOW_KEYS = 128
D_FF = 4 * D_MODEL
ROPE_THETA = 10000.0
EPS = 1e-6
NEG = -1e30
QK_SCALE = HEAD_DIM ** -0.5
LOG2_E = 1.4426950408889634

LANES = 128
COL_CHUNK = 512
VMEM_LIMIT = 48 * 1024 * 1024
VMEM_LIMIT_ROLL_HOST = 56 * 1024 * 1024


def _mm(a, b):
    return jnp.dot(a, b, preferred_element_type=F32)


def _mm_nt(a, b):
    return lax.dot_general(a, b, (((1,), (1,)), ((), ())), preferred_element_type=F32)


def _mm_tn(a, b):
    return lax.dot_general(a, b, (((0,), (0,)), ((), ())), preferred_element_type=F32)


def _rms(x, gain):
    return x * lax.rsqrt(jnp.mean(x * x, axis=-1, keepdims=True) + EPS) * gain


def _params(sem, vmem_limit=VMEM_LIMIT):
    return pltpu.CompilerParams(dimension_semantics=sem, vmem_limit_bytes=vmem_limit)


def _lane_blocks(y):
    return [y[:, j * LANES:(j + 1) * LANES] for j in range(y.shape[1] // LANES)]


def _residue_major(ref, dil):
    nblk, tm, _ = ref.shape
    per = tm // dil
    return jnp.concatenate(
        [jnp.concatenate([ref[j, pl.ds(r, per, stride=dil), :] for j in range(nblk)], axis=1)
         for r in range(dil)], axis=0)


def _store_token_order(buf, y, dil):
    per = y.shape[0] // dil
    for r in range(dil):
        for j, blk in enumerate(_lane_blocks(y[r * per:(r + 1) * per])):
            buf[j, pl.ds(r, per, stride=dil), :] = blk


def _proj_body(*refs, plan, first_tail_tile, dils):
    x_ref, g_ref, cos_ref, sin_ref, b_ref, wdec_ref, bdec_ref = refs[:7]
    w_refs = refs[7:7 + len(plan)]
    nout = sum(2 if p[4] else 1 for p in plan)
    out_refs = list(refs[7 + len(plan):7 + len(plan) + nout])
    scratch = refs[7 + len(plan) + nout:]
    tm, dm = x_ref.shape
    x = x_ref[...]
    xg = x * g_ref[...]
    inv_rms = jnp.broadcast_to(lax.rsqrt(jnp.mean(x * x, axis=-1, keepdims=True) + EPS),
                               (tm, LANES))
    if scratch:
        n_scr, tail_scr = scratch
        for j, blk in enumerate(_lane_blocks(xg) + [inv_rms]):
            n_scr[j] = blk
    rows_of = {1: xg.astype(BF16)}
    factor = {1: inv_rms}
    tables = {1: (cos_ref[...] * inv_rms, sin_ref[...] * inv_rms)}
    staged = [1]

    def prepare(d):
        held = staged[0]
        step = d // held
        assert step * held == d and step <= 4
        if held == 1:
            permuted = _residue_major(n_scr, step)
        else:
            blk = tm // held
            parts = {}
            for c in range(held):
                for s_ in range(step):
                    parts[c + held * s_] = jnp.concatenate(
                        [n_scr[j, pl.ds(c * blk + s_, blk // step, stride=step), :]
                         for j in range(n_scr.shape[0])], axis=1)
            permuted = jnp.concatenate([parts[r] for r in range(d)], axis=0)
        if d != dils[-1]:
            for j, plane in enumerate(_lane_blocks(permuted)):
                n_scr[j] = plane
            staged[0] = d
        rows_of[d] = permuted[:, :dm].astype(BF16)
        factor[d] = permuted[:, dm:]
        tables[d] = tuple(
            jnp.concatenate([t[pl.ds(r, tm // d, stride=d), :] for r in range(d)], axis=0)
            * factor[d] for t in (cos_ref, sin_ref))

    in_tail = pl.program_id(1) >= first_tail_tile
    oi = 0
    for (kind, scale, width, boff, has_tail, dil), w_ref in zip(plan, w_refs):
        out = out_refs[oi]
        oi += 1
        tail = None
        if has_tail:
            tail = out_refs[oi]
            oi += 1
        if dil not in rows_of:
            prepare(dil)
        n = rows_of[dil]
        if kind == "decay":
            ga = (_mm(n, w_ref[...]) * factor[dil]).astype(BF16)
            z = _mm(ga, wdec_ref[...]) + bdec_ref[...]
            logsig = jnp.minimum(z, 0.0) - jnp.log1p(jnp.exp(-jnp.abs(z)))
            out[...] = logsig * (1.0 / GLA_GATE_NORM)
            continue
        per = tm // dil
        for c0 in range(0, width, COL_CHUNK):
            cw = min(COL_CHUNK, width - c0)
            y = _mm(n, w_ref[:, c0:c0 + cw])
            if kind == "rope":
                cosv, sinv = tables[dil]
                heads = []
                for j in range(cw // HEAD_DIM):
                    yh = y[:, j * HEAD_DIM:(j + 1) * HEAD_DIM]
                    heads.append(yh * cosv + pltpu.roll(yh, HEAD_DIM // 2, 1) * sinv)
                y = jnp.concatenate(heads, axis=1)
            else:
                y = y * jnp.concatenate([factor[dil]] * (cw // LANES), axis=1)
            if kind == "sig":
                y = jax.nn.sigmoid(y + b_ref[:, boff + c0:boff + c0 + cw])
            if tail is not None:
                @pl.when(in_tail)
                def _():
                    if dil == 1:
                        tail[:, c0:c0 + cw] = y
                    else:
                        _store_token_order(tail_scr, y, dil)
                        for j in range(cw // LANES):
                            tail[:, c0 + j * LANES:c0 + (j + 1) * LANES] = tail_scr[j]
            if scale != 1.0:
                y = y * scale
            for r in range(dil):
                out[:, r * width + c0:r * width + c0 + cw] = y[r * per:(r + 1) * per].astype(out.dtype)


def _project(x, gain, cos, sin, segs, *, tm, tail_rows=0, bias=None, wdec=None, bdec=None,
             rolls=()):
    B, L, D = x.shape
    nL = L // tm
    ntail = tail_rows // tm
    first_tail_tile = nL - ntail
    if bias is None:
        bias = jnp.zeros((1, LANES), F32)
    if wdec is None:
        wdec = jnp.zeros((LANES, LANES), BF16)
        bdec = jnp.zeros((1, LANES), F32)

    def const2(shape):
        return pl.BlockSpec(shape, lambda b, i: (0, 0), pipeline_mode=pl.Buffered(1))

    in_specs = [
        pl.BlockSpec((None, tm, D), lambda b, i: (b, i, 0)),
        const2((1, D)),
        pl.BlockSpec((tm, LANES), lambda b, i: (i, 0)),
        pl.BlockSpec((tm, LANES), lambda b, i: (i, 0)),
        const2(bias.shape), const2(wdec.shape), const2(bdec.shape),
    ]
    plan, weights, out_shapes, out_specs = [], [], [], []
    for kind, w, scale, dtype, boff, has_tail, dil in segs:
        width = w.shape[1]
        owidth = wdec.shape[1] if kind == "decay" else width
        plan.append((kind, scale, width, boff, has_tail, dil))
        weights.append(w)
        in_specs.append(const2(w.shape))
        out_shapes.append(jax.ShapeDtypeStruct((B, L // dil, dil * owidth), dtype))
        out_specs.append(pl.BlockSpec((None, tm // dil, dil * owidth), lambda b, i: (b, i, 0)))
        if has_tail:
            out_shapes.append(jax.ShapeDtypeStruct((B, tail_rows, width), F32))
            out_specs.append(pl.BlockSpec(
                (None, tm, width),
                lambda b, i: (b, jnp.maximum(i - first_tail_tile, 0), 0),
                pipeline_mode=pl.Buffered(1)))
    dils = tuple(sorted({p[5] for p in plan}))
    scratch = []
    if dils != (1,):
        scratch = [pltpu.VMEM((D // LANES + 1, tm, LANES), F32),
                   pltpu.VMEM((COL_CHUNK // LANES, tm, LANES), F32)]
    nsteps = B * nL
    extra, extra_specs, roll_shapes, roll_specs, roll_scratch = _roll_specs(rolls, nsteps)
    body = functools.partial(_proj_body, plan=tuple(plan), first_tail_tile=first_tail_tile,
                             dils=dils)
    body = _with_rolls(body, len(in_specs), len(out_shapes), len(rolls), nsteps,
                       lambda: pl.program_id(0) * nL + pl.program_id(1))
    out = pl.pallas_call(
        body, out_shape=out_shapes + roll_shapes, grid=(B, nL),
        in_specs=in_specs + extra_specs, out_specs=out_specs + roll_specs,
        scratch_shapes=roll_scratch + scratch,
        compiler_params=_params(("arbitrary", "arbitrary"),
                                VMEM_LIMIT_ROLL_HOST if rolls else VMEM_LIMIT), name="project",
    )(x, gain, cos, sin, bias, wdec, bdec, *weights, *extra)
    return out[:len(out_shapes)], out[len(out_shapes):]


GLA_TILE = 8
GLA_SUB = 4


def _gla_level_masks(T):
    row = lax.broadcasted_iota(jnp.int32, (T, T), 0)
    col = lax.broadcasted_iota(jnp.int32, (T, T), 1)
    rid = lax.broadcasted_iota(jnp.int32, (T, 1), 0)
    levels = []
    bs = 2 * GLA_SUB
    while bs <= T:
        half = bs // 2
        second = (rid & (bs - 1)) >= half
        pair = ((row & -bs) == (col & -bs)) & ((row & (bs - 1)) >= half) & ((col & (bs - 1)) < half)
        levels.append((second, pair, bs))
        bs *= 2
    return levels


def _gla_matmuls(qf, kf, v, cum, s0, levels):
    T = qf.shape[0]
    last = cum[T - 1:T, :]
    o_inter = _mm((qf * jnp.exp2(cum)).astype(BF16), s0.astype(BF16))
    off = jnp.zeros((T, T), F32)
    for second, pair_mask, bs in levels:
        bound = []
        for j in range(T // bs):
            mid_row = j * bs + bs // 2 - 1
            mid = jnp.broadcast_to(cum[mid_row:mid_row + 1, :], (GLA_TILE, HEAD_DIM))
            bound += [mid] * (bs // GLA_TILE)
        rel = cum - jnp.concatenate(bound, axis=0)
        e = jnp.exp2(jnp.where(second, rel, -rel))
        off = jnp.where(pair_mask, _mm_nt((qf * e).astype(BF16), (kf * e).astype(BF16)), off)
    upd = _mm_tn((kf * jnp.exp2(last - cum)).astype(BF16), v)
    er = lax.broadcasted_iota(jnp.int32, (HEAD_DIM, HEAD_DIM), 0)
    ec = lax.broadcasted_iota(jnp.int32, (HEAD_DIM, HEAD_DIM), 1)
    decay_col = jnp.sum(jnp.where(er == ec, jnp.exp2(last), 0.0), axis=1, keepdims=True)
    return o_inter, off, decay_col * s0 + upd


def _gla_diagonal(qf, kf, cum, off, diag_masks):
    T = qf.shape[0]
    tiles = (T // GLA_TILE, GLA_TILE, HEAD_DIM)
    q3, k3, c3 = qf.reshape(tiles), kf.reshape(tiles), cum.reshape(tiles)
    att3 = off.reshape(T // GLA_TILE, GLA_TILE, T)
    for j in range(GLA_SUB):
        if j == 0:
            prod = q3 * k3
        else:
            prod = q3 * pltpu.roll(k3, j, 1) * jnp.exp2(c3 - pltpu.roll(c3, j, 1))
        att3 = jnp.where(diag_masks[j], jnp.sum(prod, axis=-1, keepdims=True), att3)
    return att3.reshape(T, T)


def _gla_body(q_ref, k_ref, v_ref, lf_ref, r_ref, s0_ref, hn_ref, a_ref, s_ref, *, nb):
    T = q_ref.shape[1]
    hw = HEADS * HEAD_DIM
    first = pl.program_id(1) == 0
    row = lax.broadcasted_iota(jnp.int32, (T, T), 0)
    col = lax.broadcasted_iota(jnp.int32, (T, T), 1)
    tri = jnp.where(row >= col, 1.0, 0.0).astype(BF16)
    levels = _gla_level_masks(T)
    shape3 = (T // GLA_TILE, GLA_TILE, T)
    sub_row = lax.broadcasted_iota(jnp.int32, shape3, 1)
    token = lax.broadcasted_iota(jnp.int32, shape3, 0) * GLA_TILE + sub_row
    key = lax.broadcasted_iota(jnp.int32, shape3, 2)
    diag_masks = [(key == token - j) & ((sub_row & (GLA_SUB - 1)) >= j) for j in range(GLA_SUB)]

    @pl.when(first)
    def _():
        s_ref[...] = s0_ref[...]

    cums = []
    for b in range(nb):
        lf = lf_ref[b] * LOG2_E
        hi = lf.astype(BF16)
        lo = (lf - hi.astype(F32)).astype(BF16)
        parts = _mm(tri, jnp.concatenate([hi, lo], axis=1))
        cums.append(parts[:, :hw] + parts[:, hw:])
    pairs = [(b, h) for b in range(nb) for h in range(HEADS)]
    dk = lambda h: slice(h * HEAD_DIM, (h + 1) * HEAD_DIM)
    dv = lambda h: slice(h * GLA_DV, (h + 1) * GLA_DV)
    qk = {(b, h): (q_ref[b, :, dk(h)].astype(F32), k_ref[b, :, dk(h)].astype(F32), cums[b][:, dk(h)])
          for b, h in pairs}
    stage1 = {}
    for b, h in pairs:
        qf, kf, cum = qk[b, h]
        stage1[b, h] = _gla_matmuls(qf, kf, v_ref[b, :, dv(h)], cum, s_ref[b, h], levels)
        s_ref[b, h] = stage1[b, h][2]
    att = {}
    for b, h in pairs:
        qf, kf, cum = qk[b, h]
        att[b, h] = _gla_diagonal(qf, kf, cum, stage1[b, h][1], diag_masks).astype(BF16)
    for b, h in pairs:
        o = stage1[b, h][0] + _mm(att[b, h], v_ref[b, :, dv(h)])
        r = r_ref[b, :, dv(h)].astype(F32)
        a_ref[b, :, dv(h)] = (_rms(o, hn_ref[...]) * (r * jax.nn.sigmoid(r))).astype(a_ref.dtype)


def _gla(q, k, v, lf, r, s0, head_norm, *, chunk, nb):
    B, L, hw = q.shape
    tok = lambda w: pl.BlockSpec((nb, chunk, w), lambda b, c: (b, c, 0))
    state = pl.BlockSpec((nb, HEADS, HEAD_DIM, GLA_DV), lambda b, c: (b, 0, 0, 0))
    return pl.pallas_call(
        functools.partial(_gla_body, nb=nb),
        out_shape=[jax.ShapeDtypeStruct((B, L, HEADS * GLA_DV), BF16),
                   jax.ShapeDtypeStruct(s0.shape, F32)],
        grid=(B // nb, L // chunk),
        in_specs=[tok(512), tok(512), tok(1024), tok(512), tok(1024), state,
                  pl.BlockSpec((1, GLA_DV), lambda b, c: (0, 0))],
        out_specs=[tok(1024), state],
        compiler_params=_params(("parallel", "arbitrary")), name="gla",
    )(q, k, v, lf, r, s0, head_norm)


def _band_body(q_ref, k_ref, v_ref, kp_ref, vp_ref, o_ref, lse_ref, *, tq):
    step = pl.program_id(2)
    nblk = tq // WINDOW_KEYS
    t = lax.broadcasted_iota(jnp.int32, (WINDOW_KEYS, 2 * WINDOW_KEYS), 0)
    c = lax.broadcasted_iota(jnp.int32, (WINDOW_KEYS, 2 * WINDOW_KEYS), 1)
    band = (c >= t) & (c <= t + WINDOW_KEYS)
    lane = lax.broadcasted_iota(jnp.int32, (WINDOW_KEYS, LANES), 1)
    rows = lambda j: slice(j * WINDOW_KEYS, (j + 1) * WINDOW_KEYS)
    hd = lambda h: slice(h * HEAD_DIM, (h + 1) * HEAD_DIM)

    def window(cur_ref, prev_ref, j, h):
        before = prev_ref[:, hd(h)] if j == 0 else cur_ref[rows(j - 1), hd(h)]
        return jnp.concatenate([before, cur_ref[rows(j), hd(h)]], axis=0)

    pairs = [(j, h) for j in range(nblk) for h in range(HEADS)]
    scores = [_mm_nt(q_ref[rows(j), hd(h)], window(k_ref, kp_ref, j, h)) for j, h in pairs]
    probs = []
    for (j, h), s in zip(pairs, scores):
        first_key = step * tq + (j - 1) * WINDOW_KEYS
        s = jnp.where(band & (c + first_key >= 0), s, NEG)
        m = jnp.max(s, axis=-1, keepdims=True)
        p = jnp.exp(s - m)
        den = jnp.sum(p, axis=-1, keepdims=True)
        probs.append((p.astype(BF16), den, m + jnp.log(den)))
    lse_blk = [jnp.zeros((WINDOW_KEYS, LANES), F32) for _ in range(nblk)]
    for (j, h), (p, den, lse) in zip(pairs, probs):
        o_ref[rows(j), hd(h)] = (_mm(p, window(v_ref, vp_ref, j, h)) / den).astype(o_ref.dtype)
        lse_blk[j] = jnp.where(lane == h, lse, lse_blk[j])
    for j in range(nblk):
        lse_ref[rows(j), :] = lse_blk[j][:, :HEADS]


def _band_attention(q, k, v, dil, *, tq):
    B, n, _ = q.shape
    W = HEADS * HEAD_DIM
    per = tq // WINDOW_KEYS
    cur = pl.BlockSpec((None, tq, W), lambda b, r, i: (b, i, r))
    prv = pl.BlockSpec((None, WINDOW_KEYS, W), lambda b, r, i: (b, jnp.maximum(i * per - 1, 0), r))
    o, lse = pl.pallas_call(
        functools.partial(_band_body, tq=tq),
        out_shape=[jax.ShapeDtypeStruct((B, n, dil * W), BF16),
                   jax.ShapeDtypeStruct((B, dil, n, HEADS), F32)],
        grid=(B, dil, n // tq),
        in_specs=[cur, cur, cur, prv, prv],
        out_specs=[cur, pl.BlockSpec((None, None, tq, HEADS), lambda b, r, i: (b, r, i, 0))],
        compiler_params=_params(("parallel", "parallel", "arbitrary")), name="band_attention",
    )(q, k, v, k, v)
    return o, lse.transpose(0, 2, 1, 3).reshape(B, n * dil, HEADS)


def _cross_body(q_ref, mk_ref, mv_ref, o_ref, *, nb):
    slots = mk_ref.shape[1] // HEADS
    pairs = [(b, h) for b in range(nb) for h in range(HEADS)]
    head_rows = lambda h: pl.ds(h, slots, stride=HEADS)
    lanes = lambda h: slice(h * HEAD_DIM, (h + 1) * HEAD_DIM)
    scores = [_mm_nt(q_ref[b, :, lanes(h)], mk_ref[b, head_rows(h), :].astype(BF16))
              for b, h in pairs]
    probs = []
    for s in scores:
        p = jnp.exp(s - jnp.max(s, axis=-1, keepdims=True))
        probs.append((p.astype(BF16), jnp.sum(p, axis=-1, keepdims=True)))
    for (b, h), (p, den) in zip(pairs, probs):
        pv = _mm(p, mv_ref[b, head_rows(h), :].astype(BF16))
        o_ref[b, :, lanes(h)] = (pv / den).astype(o_ref.dtype)


def _cross_attention(q, mk, mv, *, tq, nb):
    B, L, W = q.shape
    tok = pl.BlockSpec((nb, tq, W), lambda b, i: (b, i, 0))
    mem = pl.BlockSpec((nb,) + mk.shape[1:], lambda b, i: (b, 0, 0))
    return pl.pallas_call(
        functools.partial(_cross_body, nb=nb), out_shape=jax.ShapeDtypeStruct((B, L, W), BF16),
        grid=(B // nb, L // tq), in_specs=[tok, mem, mem], out_specs=tok,
        compiler_params=_params(("parallel", "parallel")), name="cross_attention",
    )(q, mk, mv)


DECODE_ROW_GROUP = 16


DECODE_ROWS = 16


def _decode_body(q_ref, kn_ref, vn_ref, ck_ref, cv_ref, o_ref, lse_ref, *, dil, nnew, nb):
    nk = dil * WINDOW_KEYS
    qrow = lax.broadcasted_iota(jnp.int32, (DECODE_ROWS, nk), 0)
    kcol = lax.broadcasted_iota(jnp.int32, (DECODE_ROWS, nk), 1)
    visible = (kcol >= qrow) if dil == 1 else ((kcol % dil) == qrow)
    new_q = lax.broadcasted_iota(jnp.int32, (nnew, 1), 0)
    head_new = lambda h: pl.ds(h, nnew, stride=HEADS)
    cached = lambda ref, b, h: ref[b, pl.ds(h, nk, stride=HEADS), :]

    pairs = [(b, h) for b in range(nb) for h in range(HEADS)]
    pad = jnp.zeros((DECODE_ROWS - nnew, LANES), F32)
    queries = {bh: q_ref[bh[0], head_new(bh[1]), :] for bh in pairs}
    scores = {bh: _mm_nt(jnp.concatenate([queries[bh], pad], axis=0).astype(BF16),
                         cached(ck_ref, *bh).astype(BF16)) for bh in pairs}
    soft = {}
    for b, h in pairs:
        s = jnp.where(visible, scores[b, h], NEG)[:nnew]
        kn = kn_ref[b, head_new(h), :]
        sn = []
        for c in range(nnew):
            ok = (new_q >= c) if dil == 1 else (new_q == c)
            sn.append(jnp.where(ok, jnp.sum(queries[b, h] * kn[c:c + 1, :], axis=-1, keepdims=True),
                                NEG))
        m = jnp.max(s, axis=-1, keepdims=True)
        for t in sn:
            m = jnp.maximum(m, t)
        p = jnp.exp(s - m)
        pn = [jnp.exp(t - m) for t in sn]
        den = jnp.sum(p, axis=-1, keepdims=True)
        for t in pn:
            den = den + t
        p_rows = jnp.concatenate([p, jnp.zeros((DECODE_ROWS - nnew, nk), F32)], axis=0)
        soft[b, h] = (p_rows.astype(BF16), pn, den, m + jnp.log(den))
    for b, h in pairs:
        p, pn, den, lse = soft[b, h]
        acc = _mm(p, cached(cv_ref, b, h).astype(BF16))[:nnew]
        vn = vn_ref[b, head_new(h), :]
        for c in range(nnew):
            acc = acc + pn[c] * vn[c:c + 1, :]
        o_ref[b, head_new(h), :] = acc / den
        lse_ref[b, head_new(h), :] = jnp.broadcast_to(lse, (nnew, LANES))


def _decode_grouped_body(q_ref, kn_ref, vn_ref, ck_ref, cv_ref, o_ref, lse_ref, *, nb):
    group = ck_ref.shape[2]
    nk = WINDOW_KEYS * GLA_TILE
    qrow = lax.broadcasted_iota(jnp.int32, (DECODE_ROWS, nk), 0)
    kcol = lax.broadcasted_iota(jnp.int32, (DECODE_ROWS, nk), 1)
    visible = (kcol % GLA_TILE) == qrow
    rows = lambda t: slice(t * GLA_TILE, (t + 1) * GLA_TILE)
    tile_of = lambda ref, b, t: ref[b, :, rows(t), :].reshape(nk, LANES).astype(BF16)
    units = [(b, t) for b in range(nb) for t in range(group // GLA_TILE)]
    pad = jnp.zeros((DECODE_ROWS - GLA_TILE, LANES), F32)
    scores = {(b, t): _mm_nt(jnp.concatenate([q_ref[b, rows(t), :], pad], axis=0).astype(BF16),
                             tile_of(ck_ref, b, t)) for b, t in units}
    soft = {}
    for b, t in units:
        s = jnp.where(visible, scores[b, t], NEG)[:GLA_TILE]
        sn = jnp.sum(q_ref[b, rows(t), :] * kn_ref[b, rows(t), :], axis=-1, keepdims=True)
        m = jnp.maximum(jnp.max(s, axis=-1, keepdims=True), sn)
        p = jnp.exp(s - m)
        pn = jnp.exp(sn - m)
        den = jnp.sum(p, axis=-1, keepdims=True) + pn
        p_rows = jnp.concatenate([p, jnp.zeros((DECODE_ROWS - GLA_TILE, nk), F32)], axis=0)
        soft[b, t] = (p_rows.astype(BF16), pn, den, m + jnp.log(den))
    for b, t in units:
        p, pn, den, lse = soft[b, t]
        acc = _mm(p, tile_of(cv_ref, b, t))[:GLA_TILE] + pn * vn_ref[b, rows(t), :]
        o_ref[b, rows(t), :] = acc / den
        lse_ref[b, rows(t), :] = jnp.broadcast_to(lse, (GLA_TILE, LANES))


def _decode_attention(q, kn, vn, ck, cv, window, dil, *, nb):
    B, nrow, _ = q.shape
    rows = window * HEADS
    small = pl.BlockSpec((nb, nrow, LANES), lambda b: (b, 0, 0))
    slot_rows = HEADS * dil
    if slot_rows > DECODE_ROW_GROUP:
        assert nrow == DECODE_ROW_GROUP
        view = lambda c: c.reshape(B, WINDOW_KEYS, slot_rows, LANES)
        big = pl.BlockSpec((nb, WINDOW_KEYS, DECODE_ROW_GROUP, LANES), lambda b: (b, 0, 0, 0))
        body = functools.partial(_decode_grouped_body, nb=nb)
    else:
        view = lambda c: c
        big = pl.BlockSpec((nb, rows, LANES), lambda b: (b, 0, 0))
        body = functools.partial(_decode_body, dil=dil, nnew=nrow // HEADS, nb=nb)
    return pl.pallas_call(
        body,
        out_shape=[jax.ShapeDtypeStruct((B, nrow, LANES), F32),
                   jax.ShapeDtypeStruct((B, nrow, LANES), F32)],
        grid=(B // nb,), in_specs=[small, small, small, big, big],
        out_specs=[small, small],
        compiler_params=_params(("parallel",)), name="decode_attention",
    )(q, kn, vn, view(ck), view(cv))


def _roll_copies(step, nsteps, old, new, out, buf, sems):
    total, rows, _ = old.shape
    shift = new.shape[1]
    per = total // nsteps
    batches = pl.ds(step * per, per)
    slot = step % 2
    stage = buf.at[slot]
    reads = [
        pltpu.make_async_copy(old.at[batches, pl.ds(shift, rows - shift), :],
                              stage.at[:, pl.ds(0, rows - shift), :], sems.at[slot, 0]),
        pltpu.make_async_copy(new.at[batches], stage.at[:, pl.ds(rows - shift, shift), :],
                              sems.at[slot, 1]),
    ]
    return reads, pltpu.make_async_copy(stage, out.at[batches], sems.at[slot, 2])


def _with_rolls(body, nfixed_in, nfixed_out, nrolls, nsteps, step_fn):
    if nrolls == 0:
        return body

    def wrapped(*refs):
        ins = refs[:nfixed_in]
        pairs = refs[nfixed_in:nfixed_in + 2 * nrolls]
        base = nfixed_in + 2 * nrolls
        outs = refs[base:base + nfixed_out]
        rolled = refs[base + nfixed_out:base + nfixed_out + nrolls]
        staging = refs[base + nfixed_out + nrolls:base + nfixed_out + 3 * nrolls]
        rest = refs[base + nfixed_out + 3 * nrolls:]
        step = step_fn()
        args = [(pairs[2 * t], pairs[2 * t + 1], rolled[t], staging[2 * t + 1], staging[2 * t])
                for t in range(nrolls)]

        @pl.when(step >= 2)
        def _():
            for a in args:
                _roll_copies(step - 2, nsteps, *a)[1].wait()

        current = [_roll_copies(step, nsteps, *a) for a in args]
        for reads, _ in current:
            for c in reads:
                c.start()
        body(*ins, *outs, *rest)
        for reads, write in current:
            for c in reads:
                c.wait()
            write.start()

        @pl.when(step == nsteps - 1)
        def _():
            for a in args:
                _roll_copies(step - 1, nsteps, *a)[1].wait()
            for _, write in current:
                write.wait()

    return wrapped


def _roll_specs(rolls, nsteps):
    any_spec = pl.BlockSpec(memory_space=pl.ANY)
    extra, shapes, scratch = [], [], []
    for old, new in rolls:
        assert old.shape[0] % nsteps == 0 and nsteps >= 2
        extra += [old, new]
        shapes.append(jax.ShapeDtypeStruct(old.shape, old.dtype))
        scratch += [pltpu.SemaphoreType.DMA((2, 3)),
                    pltpu.VMEM((2, old.shape[0] // nsteps) + old.shape[1:], old.dtype)]
    return extra, [any_spec] * len(extra), shapes, [any_spec] * len(shapes), scratch


def _merge_body(x_ref, a_ref, o1_ref, o2_ref, o3_ref, l_ref, ox_ref, g_ref,
                wa_ref, wb_ref, wc_ref, wo_ref, h_ref, *scratch, dils):
    tm = x_ref.shape[0]
    hw = HEADS * HEAD_DIM
    groups, si = [], 0
    for o_ref, d in zip((o1_ref, o2_ref, o3_ref), dils):
        if d == 1:
            groups.append([o_ref[:, h * HEAD_DIM:(h + 1) * HEAD_DIM] for h in range(HEADS)])
            continue
        buf = scratch[si]
        si += 1
        for r in range(d):
            for h in range(HEADS):
                lanes = slice(r * hw + h * HEAD_DIM, r * hw + (h + 1) * HEAD_DIM)
                buf[h, pl.ds(r, tm // d, stride=d), :] = o_ref[:, lanes].astype(F32)
        groups.append([buf[h] for h in range(HEADS)])
    l1, l2, l3 = (l_ref[:, g * LANES:g * LANES + HEADS] for g in range(3))
    lmax = jnp.maximum(jnp.maximum(l1, l2), l3)
    e1, e2, e3 = jnp.exp(l1 - lmax), jnp.exp(l2 - lmax), jnp.exp(l3 - lmax)
    inv = 1.0 / (e1 + e2 + e3)
    heads = []
    for h in range(HEADS):
        heads.append((e1[:, h:h + 1] * inv[:, h:h + 1]) * groups[0][h].astype(F32)
                     + (e2[:, h:h + 1] * inv[:, h:h + 1]) * groups[1][h].astype(F32)
                     + (e3[:, h:h + 1] * inv[:, h:h + 1]) * groups[2][h].astype(F32))
    o_dil = jnp.concatenate(heads, axis=1).astype(BF16)
    ya = _mm(a_ref[...], wa_ref[...])
    yb = _mm(o_dil, wb_ref[...])
    yc = _mm(ox_ref[...], wc_ref[...])
    d = D_MODEL
    mix = (g_ref[:, 0:d].astype(F32) * ya + g_ref[:, d:2 * d].astype(F32) * yb
           + g_ref[:, 2 * d:3 * d].astype(F32) * yc)
    h_ref[...] = x_ref[...] + _mm(mix.astype(BF16), wo_ref[...])


def _merge(x, a, o_groups, lse_groups, ox, gates, wa, wb, wc, wo, *, tm, dils, rolls=()):
    T = x.shape[0]
    nsteps = T // tm
    hw = HEADS * HEAD_DIM
    tok = lambda w: pl.BlockSpec((tm, w), lambda i: (i, 0))
    grp = lambda d: pl.BlockSpec((tm // d, d * hw), lambda i: (i, 0))
    full = lambda w: pl.BlockSpec(w.shape, lambda i: (0, 0))
    extra, extra_specs, roll_shapes, roll_specs, roll_scratch = _roll_specs(rolls, nsteps)
    body = _with_rolls(functools.partial(_merge_body, dils=dils), 12, 1, len(rolls), nsteps,
                       lambda: pl.program_id(0))
    lse_all = jnp.concatenate(
        [jnp.pad(l, ((0, 0), (0, LANES - HEADS))) for l in lse_groups], axis=1)
    out = pl.pallas_call(
        body, out_shape=[jax.ShapeDtypeStruct((T, D_MODEL), F32)] + roll_shapes, grid=(nsteps,),
        in_specs=[tok(D_MODEL), tok(HEADS * GLA_DV), grp(dils[0]), grp(dils[1]), grp(dils[2]),
                  tok(3 * LANES), tok(hw), tok(3 * D_MODEL),
                  full(wa), full(wb), full(wc), full(wo)] + extra_specs,
        out_specs=[tok(D_MODEL)] + roll_specs,
        scratch_shapes=roll_scratch + [pltpu.VMEM((HEADS, tm, HEAD_DIM), F32)
                                       for d in dils if d > 1],
        compiler_params=_params(("arbitrary",)), name="merge",
    )(x, a, *o_groups, lse_all, ox, gates, wa, wb, wc, wo, *extra)
    return out[0], out[1:]


def _ffn_body(h_ref, gn_ref, wu_ref, wd_ref, gf_ref, y_ref):
    h = h_ref[...]
    n = (h * gn_ref[...]).astype(BF16)
    inv_ms = 1.0 / (jnp.mean(h * h, axis=-1, keepdims=True) + EPS)
    acc = None
    for c0 in range(0, D_FF, COL_CHUNK):
        u = jnp.maximum(_mm(n, wu_ref[:, c0:c0 + COL_CHUNK]), 0.0)
        part = _mm((u * u).astype(BF16), wd_ref[c0:c0 + COL_CHUNK, :])
        acc = part if acc is None else acc + part
    y_ref[...] = _rms(h + inv_ms * acc, gf_ref[...])


def _ffn(h, gain_ffn, wu, wd, gain_final, *, tm, rolls=()):
    T = h.shape[0]
    nsteps = T // tm
    tok = pl.BlockSpec((tm, D_MODEL), lambda i: (i, 0))
    full = lambda w: pl.BlockSpec(w.shape, lambda i: (0, 0), pipeline_mode=pl.Buffered(1))
    extra, extra_specs, roll_shapes, roll_specs, roll_scratch = _roll_specs(rolls, nsteps)
    body = _with_rolls(_ffn_body, 5, 1, len(rolls), nsteps, lambda: pl.program_id(0))
    out = pl.pallas_call(
        body, out_shape=[jax.ShapeDtypeStruct((T, D_MODEL), F32)] + roll_shapes, grid=(nsteps,),
        in_specs=[tok, full(gain_ffn), full(wu), full(wd), full(gain_final)] + extra_specs,
        out_specs=[tok] + roll_specs, scratch_shapes=roll_scratch,
        compiler_params=_params(("arbitrary",)), name="ffn",
    )(h, gain_ffn, wu, wd, gain_final, *extra)
    return out[0], out[1:]


def _rope_angles(pos):
    half = HEAD_DIM // 2
    inv = ROPE_THETA ** (-jnp.arange(half, dtype=F32) / half)
    ang = pos.astype(F32)[:, None] * inv[None, :]
    return jnp.cos(ang), jnp.sin(ang)


def _rope_tables(pos):
    cos, sin = _rope_angles(pos)
    return jnp.concatenate([cos, cos], axis=-1), jnp.concatenate([-sin, sin], axis=-1)


def _rope_tables_range(length, block=LANES):
    cos_hi, sin_hi = _rope_angles(jnp.arange(0, length, block))
    cos_lo, sin_lo = _rope_angles(jnp.arange(block))
    half = HEAD_DIM // 2
    cos = (cos_hi[:, None] * cos_lo[None] - sin_hi[:, None] * sin_lo[None]).reshape(length, half)
    sin = (sin_hi[:, None] * cos_lo[None] + cos_hi[:, None] * sin_lo[None]).reshape(length, half)
    return jnp.concatenate([cos, cos], axis=-1), jnp.concatenate([-sin, sin], axis=-1)


def _split_weights(w_in, w_decay):
    hw = HEADS * HEAD_DIM
    sizes = [hw, hw, HEADS * GLA_DV, HEADS * GLA_DV, GLA_RANK] + [hw] * 9 + [hw, 3 * D_MODEL]
    offs = [0]
    for s in sizes:
        offs.append(offs[-1] + s)
    cols = [w_in[:, offs[j]:offs[j + 1]].astype(BF16) for j in range(len(sizes))]
    cols[4] = jnp.pad(cols[4], ((0, 0), (0, LANES - GLA_RANK)))
    wdec = jnp.pad(w_decay.astype(BF16), ((0, LANES - GLA_RANK), (0, 0)))
    return cols, wdec


def _layer(x, rope, tail_rows, cols, wdec, b_decay, b_gate, norm_mix, tm, dils,
           rolls=((), (), ())):
    cos, sin = rope
    gain = norm_mix[None, :]
    (gq, gk, gv, gr, lf), rolled_a = _project(
        x, gain, cos, sin,
        [("lin", cols[0], QK_SCALE, BF16, 0, False, 1), ("lin", cols[1], 1.0, BF16, 0, False, 1),
         ("lin", cols[2], 1.0, BF16, 0, False, 1), ("lin", cols[3], 1.0, BF16, 0, False, 1),
         ("decay", cols[4], 1.0, F32, 0, False, 1)],
        tm=tm[0], wdec=wdec, bdec=b_decay[None, :], rolls=rolls[0])
    dil_segs = []
    for g in range(3):
        dil_segs += [("rope", cols[5 + 3 * g], QK_SCALE, BF16, 0, False, dils[g]),
                     ("rope", cols[6 + 3 * g], 1.0, BF16, 0, True, dils[g]),
                     ("lin", cols[7 + 3 * g], 1.0, BF16, 0, True, dils[g])]
    dil_out, rolled_b = _project(x, gain, cos, sin, dil_segs, tm=tm[1], tail_rows=tail_rows,
                                 rolls=rolls[1])
    (xq, gates), rolled_c = _project(
        x, gain, cos, sin,
        [("lin", cols[14], QK_SCALE, BF16, 0, False, 1), ("sig", cols[15], 1.0, BF16, 0, False, 1)],
        tm=tm[2], bias=b_gate[None, :], rolls=rolls[2])
    dil = [dil_out[5 * g:5 * g + 5] for g in range(3)]
    return (gq, gk, gv, gr, lf), dil, xq, gates, (rolled_a, rolled_b, rolled_c)


def _finish(x2d, a, o_groups, lse_groups, ox, gates, post, tm, dils, merge_rolls=(),
            ffn_rolls=()):
    wa, wb, wc, wo, gain_ffn, wu, wd, gain_final = post
    flat = lambda t: t.reshape(-1, t.shape[-1])
    h, rolled_m = _merge(x2d, flat(a), [flat(t) for t in o_groups], [flat(t) for t in lse_groups],
                         flat(ox), flat(gates), wa, wb, wc, wo, tm=tm, dils=dils,
                         rolls=merge_rolls)
    y, rolled_f = _ffn(h, gain_ffn, wu, wd, gain_final, tm=tm, rolls=ffn_rolls)
    return y, rolled_m, rolled_f


def kernel(x_prompt, x_sample, mem_prompt, state_gla, cache_dil1_k, cache_dil1_v, cache_dil2_k, cache_dil2_v, cache_dil3_k, cache_dil3_v, cache_mem_k, cache_mem_v, norm_mix, w_in, b_gate, w_decay, b_decay, gla_head_norm, w_proj_gla, w_proj_dil, w_proj_x, norm_mem, w_mem_kv, w_out, norm_ffn, w_ffn_up, w_ffn_down, norm_final):
    B, L, D = x_prompt.shape
    SB, SL, _ = x_sample.shape
    depth = w_in.shape[0]
    assert depth == 1, "single trunk layer"
    i = 0
    cols, wdec = _split_weights(w_in[i], w_decay[i])
    post = (w_proj_gla[i].astype(BF16), w_proj_dil[i].astype(BF16), w_proj_x[i].astype(BF16),
            w_out[i].astype(BF16), norm_ffn[i][None, :], w_ffn_up[i].astype(BF16),
            w_ffn_down[i].astype(BF16), norm_final[None, :])
    head_norm = gla_head_norm[i][None, :]
    hw = HEADS * HEAD_DIM
    max_window = DIL_PATTERNS[-1][0]

    mem_rows = lambda t: t.reshape(t.shape[0], N_MEM * HEADS, HEAD_DIM)

    T = SB * SL
    pos_s = jnp.tile(PAST_LEN + jnp.arange(SL), SB)
    (gq, gk, gv, gr, lf), dil, xq, gates, _ = _layer(
        x_sample.reshape(1, T, D), _rope_tables(pos_s), T, cols, wdec, b_decay[i], b_gate[i],
        norm_mix[i], (T, T, T), (1, 1, 1))
    pad_rows = 16 - SL
    per_batch = lambda t: jnp.pad(t.reshape(SB, SL, t.shape[-1]), ((0, 0), (0, pad_rows), (0, 0)))
    a_s, state_s = _gla(per_batch(gq), per_batch(gk), per_batch(gv), per_batch(lf), per_batch(gr),
                        state_gla[i], head_norm, chunk=16, nb=8)
    a_s = a_s[:, :SL]
    caches = ((cache_dil1_k[i], cache_dil1_v[i]), (cache_dil2_k[i], cache_dil2_v[i]),
              (cache_dil3_k[i], cache_dil3_v[i]))
    o_groups, lse_groups, roll_pairs = [], [], []
    rows = lambda t: t.astype(F32).reshape(SB, SL * HEADS, HEAD_DIM)
    for (win, dl), (dq, dk, dk_tail, dv, dv_tail), (ck, cv), nb in zip(
            DIL_PATTERNS, dil, caches, (4, 4, 4)):
        ck, cv = (c.reshape(SB, win * HEADS, HEAD_DIM) for c in (ck, cv))
        o_g, lse_g = _decode_attention(rows(dq), rows(dk_tail), rows(dv_tail), ck, cv, win, dl,
                                       nb=nb)
        o_groups.append(o_g.reshape(SB, SL, hw).astype(BF16))
        lse_groups.append(lse_g[:, :, 0].reshape(SB, SL, HEADS))
        roll_pairs += [(ck, rows(dk_tail)), (cv, rows(dv_tail))]
    ox = _cross_attention(per_batch(xq), mem_rows(cache_mem_k[i]), mem_rows(cache_mem_v[i]),
                          tq=16, nb=8)[:, :SL]
    y_sample, _, _ = _finish(x_sample.reshape(T, D), a_s, o_groups, lse_groups, ox, gates, post,
                             T, (1, 1, 1))
    y_sample = y_sample.reshape(SB, SL, D)

    d1k, d1v, d2k, d2v, d3k, d3v = roll_pairs
    dils_p = tuple(dl for _, dl in DIL_PATTERNS)
    (gq, gk, gv, gr, lf), dil, xq, gates, ((r_d2v,), (r_d3v,), (r_d2k,)) = _layer(
        x_prompt, _rope_tables_range(L), max_window, cols, wdec, b_decay[i], b_gate[i],
        norm_mix[i], (1024, 512, 1024), dils_p, rolls=([d2v], [d3v], [d2k]))
    a_p, state_p = _gla(gq, gk, gv, lf, gr, jnp.zeros((B, HEADS, HEAD_DIM, GLA_DV), F32),
                        head_norm, chunk=128, nb=1)
    o_groups, lse_groups, bufs_p = [], [], []
    for (win, dl), (dq, dk, dk_tail, dv, dv_tail) in zip(DIL_PATTERNS, dil):
        o_g, lse_g = _band_attention(dq, dk, dv, dl, tq=512)
        o_groups.append(o_g)
        lse_groups.append(lse_g)
        keep = min(win, L)
        bufs_p += [dk_tail[:, max_window - keep:].reshape(1, B, keep, HEADS, HEAD_DIM),
                   dv_tail[:, max_window - keep:].reshape(1, B, keep, HEADS, HEAD_DIM)]
    zeros_tab = jnp.zeros((N_MEM, LANES), F32)
    w_mem = w_mem_kv[i].astype(BF16)
    (mk, mv), _ = _project(
        mem_prompt, norm_mem[i][None, :], zeros_tab, zeros_tab,
        [("lin", w_mem[:, :hw], 1.0, F32, 0, False, 1), ("lin", w_mem[:, hw:], 1.0, F32, 0, False, 1)],
        tm=N_MEM)
    ox = _cross_attention(xq, mem_rows(mk), mem_rows(mv), tq=512, nb=1)
    y_prompt, _, (r_d3k, r_d1k, r_d1v) = _finish(
        x_prompt.reshape(B * L, D), a_p, o_groups, lse_groups, ox, gates, post, 512, dils_p,
        ffn_rolls=[d3k, d1k, d1v])
    y_prompt = y_prompt.reshape(B, L, D)
    bufs_s = [c.reshape(1, SB, c.shape[1] // HEADS, HEADS, HEAD_DIM)
              for c in (r_d1k, r_d1v, r_d2k, r_d2v, r_d3k, r_d3v)]

    return (y_prompt, y_sample, state_p[None], *bufs_p,
            mk.reshape(1, B, N_MEM, HEADS, HEAD_DIM), mv.reshape(1, B, N_MEM, HEADS, HEAD_DIM),
            state_s[None], *bufs_s)
```
